```python
import jax, jax.numpy as jnp
from jax import lax
import numpy as np

D_MODEL = 2048
BATCH = 8
SEQ = 4096
DEPTH = 1

CHUNK = 64
N_MEM = 256
HG_WIDTH = D_MODEL // 2
HG_KDIM = 128
HG_HEADS = HG_WIDTH // HG_KDIM
HG_VDIM = HG_WIDTH // HG_HEADS
LRU_WIDTH = D_MODEL // 2
LRU_BLOCKS = 8
LRU_BLOCK = LRU_WIDTH // LRU_BLOCKS
CONV_WIDTH = 4
LRU_C = 8.0
XA_HEADS = 4
XA_HEAD_DIM = D_MODEL // XA_HEADS
D_FF = 256 * ((8 * D_MODEL // 3 + 255) // 256)
N_BRANCH = 2
IN_COLS = 4 * HG_WIDTH + 2 * LRU_WIDTH + N_BRANCH * D_MODEL
EPS = 1e-6

kernel_name = 'hybrid_hgrn2_rglru_gated_merge_macaron'


def rms_norm(x, g):
    xf = x.astype(jnp.float32)
    y = xf * lax.rsqrt(jnp.mean(xf * xf, axis=-1, keepdims=True) + EPS)
    return (y * g.astype(jnp.float32)).astype(x.dtype)


def swiglu(h, w_up, w_down):
    gate, up = jnp.split(h @ w_up, 2, axis=-1)
    return (jax.nn.silu(gate) * up) @ w_down


def hgrn2_chunked(q, f_logit, v, lb):
    B, S, H, K = q.shape
    V = v.shape[-1]
    nc = S // CHUNK
    qf = jax.nn.silu(q.astype(jnp.float32))
    f = lb + (1.0 - lb) * jax.nn.sigmoid(f_logit.astype(jnp.float32))
    log_f = jnp.log(f)
    k = 1.0 - f
    vf = v.astype(jnp.float32)

    def to_chunks(t):
        return t.reshape(B, nc, CHUNK, H, t.shape[-1]).transpose(1, 0, 3, 2, 4)

    tri = jnp.tril(jnp.ones((CHUNK, CHUNK), dtype=bool))

    def step(state, inp):
        qc, kc, vc, gc = inp
        b = jnp.cumsum(gc, axis=2)
        o_inter = jnp.einsum('bhtk,bhkv->bhtv', qc * jnp.exp(b), state)
        rel = b[:, :, :, None, :] - b[:, :, None, :, :]
        decay = jnp.where(tri[None, None, :, :, None], jnp.exp(jnp.minimum(rel, 0.0)), 0.0)
        scores = jnp.einsum('bhtk,bhsk,bhtsk->bhts', qc, kc, decay)
        o_intra = jnp.einsum('bhts,bhsv->bhtv', scores, vc)
        b_last = b[:, :, -1, :]
        k_dec = kc * jnp.exp(b_last[:, :, None, :] - b)
        state = jnp.exp(b_last)[..., None] * state + jnp.einsum('bhsk,bhsv->bhkv', k_dec, vc)
        return state, o_inter + o_intra

    state0 = jnp.zeros((B, H, K, V), jnp.float32)
    _, o = lax.scan(step, state0, (to_chunks(qf), to_chunks(k), to_chunks(vf), to_chunks(log_f)))
    return o.transpose(1, 0, 3, 2, 4).reshape(B, S, H, V)


def rglru_branch(xb, gate, conv_w, conv_b, wa, ba, wx, bx, lam):
    B, S, W = xb.shape
    xp = jnp.pad(xb, ((0, 0), (CONV_WIDTH - 1, 0), (0, 0)))
    xc = conv_b
    for j in range(CONV_WIDTH):
        xc = xc + xp[:, CONV_WIDTH - 1 - j:CONV_WIDTH - 1 - j + S, :] * conv_w[j]
    blocks = xc.reshape(B, S, LRU_BLOCKS, LRU_BLOCK)
    r = jax.nn.sigmoid((jnp.einsum('bsnh,nhk->bsnk', blocks, wa).reshape(B, S, W) + ba).astype(jnp.float32))
    i = jax.nn.sigmoid((jnp.einsum('bsnh,nhk->bsnk', blocks, wx).reshape(B, S, W) + bx).astype(jnp.float32))
    log_a = LRU_C * r * jax.nn.log_sigmoid(lam.astype(jnp.float32))
    a = jnp.exp(log_a)
    mult = jnp.sqrt(-jnp.expm1(2.0 * log_a))
    mult = jnp.where((jnp.arange(S) == 0)[None, :, None], 1.0, mult)
    u = xc.astype(jnp.float32) * i * mult

    def combine(c1, c2):
        a1, b1 = c1
        a2, b2 = c2
        return (a1 * a2, a2 * b1 + b2)

    _, h = lax.associative_scan(combine, (a, u), axis=1)
    return (h * jax.nn.gelu(gate.astype(jnp.float32))).astype(xb.dtype)


def cross_attention(h, m, wq, wkv, wo):
    B, S, D = h.shape
    M = m.shape[1]
    q = (h @ wq).reshape(B, S, XA_HEADS, XA_HEAD_DIM)
    k, v = jnp.split(m @ wkv, 2, axis=-1)
    k = k.reshape(B, M, XA_HEADS, XA_HEAD_DIM)
    v = v.reshape(B, M, XA_HEADS, XA_HEAD_DIM)
    s = jnp.einsum('bshd,bmhd->bhsm', q, k).astype(jnp.float32) * (XA_HEAD_DIM ** -0.5)
    p = jax.nn.softmax(s, axis=-1).astype(v.dtype)
    o = jnp.einsum('bhsm,bmhd->bshd', p, v).reshape(B, S, D)
    return o @ wo


def setup_inputs(seed: int = 0) -> dict:
    key = jax.random.key(seed)
    ks = jax.random.split(key, 32)
    f32 = jnp.float32

    def nrm(k, shape, fan_in):
        return jax.random.normal(k, shape, f32) * (fan_in ** -0.5)

    def gain(k, shape):
        return 1.0 + 0.05 * jax.random.normal(k, shape, f32)

    def bias(k, shape):
        return 0.01 * jax.random.normal(k, shape, f32)

    u = jax.random.uniform(ks[16], (DEPTH, LRU_WIDTH), f32, minval=0.9, maxval=0.999)
    p = u ** (1.0 / LRU_C)
    lam = jnp.log(p) - jnp.log1p(-p)
    return {
        'x': jax.random.normal(ks[0], (BATCH, SEQ, D_MODEL), f32),
        'mem': jax.random.normal(ks[1], (BATCH, N_MEM, D_MODEL), f32),
        'ffn1_norm': gain(ks[2], (DEPTH, D_MODEL)),
        'ffn1_w_up': nrm(ks[3], (DEPTH, D_MODEL, 2 * D_FF), D_MODEL),
        'ffn1_w_down': nrm(ks[4], (DEPTH, D_FF, D_MODEL), D_FF),
        'mix_norm': gain(ks[5], (DEPTH, D_MODEL)),
        'w_in': nrm(ks[6], (DEPTH, D_MODEL, IN_COLS), D_MODEL),
        'b_gate': bias(ks[7], (DEPTH, N_BRANCH, D_MODEL)),
        'hgrn_lb_logits': 0.5 * jax.random.normal(ks[8], (DEPTH + 1, HG_WIDTH), f32),
        'hgrn_norm': gain(ks[9], (DEPTH, HG_WIDTH)),
        'conv_w': 0.5 * jax.random.normal(ks[10], (DEPTH, CONV_WIDTH, LRU_WIDTH), f32),
        'conv_b': bias(ks[11], (DEPTH, LRU_WIDTH)),
        'lru_wa': nrm(ks[12], (DEPTH, LRU_BLOCKS, LRU_BLOCK, LRU_BLOCK), LRU_BLOCK),
        'lru_ba': bias(ks[13], (DEPTH, LRU_WIDTH)),
        'lru_wx': nrm(ks[14], (DEPTH, LRU_BLOCKS, LRU_BLOCK, LRU_BLOCK), LRU_BLOCK),
        'lru_bx': bias(ks[15], (DEPTH, LRU_WIDTH)),
        'lru_lambda': lam,
        'w_branch_a': nrm(ks[17], (DEPTH, HG_WIDTH, D_MODEL), HG_WIDTH),
        'w_branch_b': nrm(ks[18], (DEPTH, LRU_WIDTH, D_MODEL), LRU_WIDTH),
        'w_out': nrm(ks[19], (DEPTH, D_MODEL, D_MODEL), D_MODEL),
        'xattn_norm': gain(ks[20], (DEPTH, D_MODEL)),
        'mem_norm': gain(ks[21], (DEPTH, D_MODEL)),
        'xattn_wq': nrm(ks[22], (DEPTH, D_MODEL, D_MODEL), D_MODEL),
        'xattn_wkv': nrm(ks[23], (DEPTH, D_MODEL, 2 * D_MODEL), D_MODEL),
        'xattn_wo': nrm(ks[24], (DEPTH, D_MODEL, D_MODEL), D_MODEL),
        'ffn2_norm': gain(ks[25], (DEPTH, D_MODEL)),
        'ffn2_w_up': nrm(ks[26], (DEPTH, D_MODEL, 2 * D_FF), D_MODEL),
        'ffn2_w_down': nrm(ks[27], (DEPTH, D_FF, D_MODEL), D_FF),
        'final_norm': gain(ks[28], (D_MODEL,)),
    }


def reference(x, mem, ffn1_norm, ffn1_w_up, ffn1_w_down, mix_norm, w_in, b_gate,
              hgrn_lb_logits, hgrn_norm, conv_w, conv_b, lru_wa, lru_ba, lru_wx, lru_bx,
              lru_lambda, w_branch_a, w_branch_b, w_out, xattn_norm, mem_norm, xattn_wq,
              xattn_wkv, xattn_wo, ffn2_norm, ffn2_w_up, ffn2_w_down, final_norm):
    B, S, D = x.shape
    lb_table = jnp.cumsum(jax.nn.softmax(hgrn_lb_logits.astype(jnp.float32), axis=0), axis=0)
    splits = [HG_WIDTH, 2 * HG_WIDTH, 3 * HG_WIDTH, 4 * HG_WIDTH,
              4 * HG_WIDTH + LRU_WIDTH, 4 * HG_WIDTH + 2 * LRU_WIDTH]
    for l in range(DEPTH):
        x = x + 0.5 * swiglu(rms_norm(x, ffn1_norm[l]), ffn1_w_up[l], ffn1_w_down[l])

        h = rms_norm(x, mix_norm[l])
        proj = h @ w_in[l]
        q_a, f_a, i_a, g_a, x_b, gate_b, gates = jnp.split(proj, splits, axis=-1)

        lb = lb_table[l].reshape(HG_HEADS, HG_KDIM)
        o_a = hgrn2_chunked(q_a.reshape(B, S, HG_HEADS, HG_KDIM),
                            f_a.reshape(B, S, HG_HEADS, HG_KDIM),
                            i_a.reshape(B, S, HG_HEADS, HG_VDIM), lb)
        o_a = rms_norm(o_a, hgrn_norm[l].reshape(HG_HEADS, HG_VDIM))
        y_a = (o_a.reshape(B, S, HG_WIDTH) * jax.nn.silu(g_a.astype(jnp.float32))).astype(x.dtype)

        y_b = rglru_branch(x_b, gate_b, conv_w[l], conv_b[l], lru_wa[l], lru_ba[l],
                           lru_wx[l], lru_bx[l], lru_lambda[l])

        g = jax.nn.sigmoid(gates.reshape(B, S, N_BRANCH, D) + b_gate[l])
        merged = g[:, :, 0, :] * (y_a @ w_branch_a[l]) + g[:, :, 1, :] * (y_b @ w_branch_b[l])
        x = x + merged @ w_out[l]

        x = x + cross_attention(rms_norm(x, xattn_norm[l]), rms_norm(mem, mem_norm[l]),
                                xattn_wq[l], xattn_wkv[l], xattn_wo[l])

        x = x + 0.5 * swiglu(rms_norm(x, ffn2_norm[l]), ffn2_w_up[l], ffn2_w_down[l])
    return rms_norm(x, final_norm)
```

```python
import functools

import jax
import jax.numpy as jnp
from jax import lax
from jax.experimental import pallas as pl
from jax.experimental.pallas import tpu as pltpu

EPS = 1e-6
HG_KDIM = 128
XA_HEADS = 4
LRU_C = 8.0
CONV_WIDTH = 4
LANES = 128
SUBLANES = 8
V7X_VMEM_BYTES = 64 * 1024 * 1024
VMEM_LIMIT = V7X_VMEM_BYTES - 8 * 1024 * 1024

HG_CHUNK = 64
HG_SUB = 16

F32 = jnp.float32
BF16 = jnp.bfloat16


def _params(*semantics):
    return pltpu.CompilerParams(dimension_semantics=semantics, vmem_limit_bytes=VMEM_LIMIT)


def _tile(n, pref):
    t = min(n, pref)
    while n % t:
        t //= 2
    return t


def _rms(xf, g):
    ms = jnp.mean(xf * xf, axis=-1, keepdims=True)
    return xf * lax.rsqrt(ms + EPS) * g


def _dot(a, b):
    return jnp.dot(a, b, preferred_element_type=F32)


def _dot_nt(a, b):
    return lax.dot_general(a, b, (((1,), (1,)), ((), ())), preferred_element_type=F32)


def _resident(shape):
    return pl.BlockSpec(shape, lambda *_: (0,) * len(shape), pipeline_mode=pl.Buffered(1))


def _ffn_kernel(x_ref, g_ref, wg_ref, wu_ref, wd_ref, fg_ref, o_ref, hn_ref, acc_ref, *, final_norm):
    j = pl.program_id(1)

    @pl.when(j == 0)
    def _():
        hn_ref[...] = _rms(x_ref[...], g_ref[...]).astype(BF16)
        acc_ref[...] = jnp.zeros_like(acc_ref)

    h = hn_ref[...]
    gate = _dot(h, wg_ref[...])
    up = _dot(h, wu_ref[...])
    act = (gate * jax.nn.sigmoid(gate) * up).astype(BF16)
    acc_ref[...] += _dot(act, wd_ref[...])

    @pl.when(j == pl.num_programs(1) - 1)
    def _():
        y = x_ref[...] + 0.5 * acc_ref[...]
        if final_norm:
            y = _rms(y, fg_ref[...])
        o_ref[...] = y


def _ffn(x2, gain, w_up, w_down, final_gain, *, final_norm, tm_pref=512, tf_pref=512):
    m, d = x2.shape
    dff = w_down.shape[0]
    tm = _tile(m, tm_pref)
    tf = _tile(dff, tf_pref)
    nf = dff // tf
    return pl.pallas_call(
        functools.partial(_ffn_kernel, final_norm=final_norm),
        grid=(m // tm, nf),
        in_specs=[
            pl.BlockSpec((tm, d), lambda i, j: (i, 0)),
            pl.BlockSpec((1, d), lambda i, j: (0, 0)),
            pl.BlockSpec((d, tf), lambda i, j: (0, j)),
            pl.BlockSpec((d, tf), lambda i, j: (0, j + nf)),
            pl.BlockSpec((tf, d), lambda i, j: (j, 0)),
            pl.BlockSpec((1, d), lambda i, j: (0, 0)),
        ],
        out_specs=pl.BlockSpec((tm, d), lambda i, j: (i, 0)),
        out_shape=jax.ShapeDtypeStruct((m, d), F32),
        scratch_shapes=[pltpu.VMEM((tm, d), BF16), pltpu.VMEM((tm, d), F32)],
        compiler_params=_params("parallel", "arbitrary"),
        name="ffn_final" if final_norm else "ffn",
    )(x2, gain, w_up, w_up, w_down, final_gain)


def _norm_proj_kernel(x_ref, g_ref, w_ref, o_ref, hn_ref):
    @pl.when(pl.program_id(1) == 0)
    def _():
        hn_ref[...] = _rms(x_ref[...], g_ref[...]).astype(BF16)

    o_ref[...] = _dot(hn_ref[...], w_ref[...]).astype(o_ref.dtype)


def _norm_proj(x2, gain, w, out_dtype, *, name, tm_pref=1024, tn_pref=1024):
    m, d = x2.shape
    n = w.shape[1]
    tm = _tile(m, tm_pref)
    tn = _tile(n, tn_pref)
    return pl.pallas_call(
        _norm_proj_kernel,
        grid=(m // tm, n // tn),
        in_specs=[
            pl.BlockSpec((tm, d), lambda i, j: (i, 0)),
            pl.BlockSpec((1, d), lambda i, j: (0, 0)),
            pl.BlockSpec((d, tn), lambda i, j: (0, j)),
        ],
        out_specs=pl.BlockSpec((tm, tn), lambda i, j: (i, j)),
        out_shape=jax.ShapeDtypeStruct((m, n), out_dtype),
        scratch_shapes=[pltpu.VMEM((tm, d), BF16)],
        compiler_params=_params("parallel", "arbitrary"),
        name=name,
    )(x2, gain, w)


def _hgrn_kernel(q_ref, f_ref, v_ref, g_ref, lbl_ref, gn_ref, y_ref,
                 st_ref, qf_ref, kk_ref, b_ref, o_ref, e_ref, *, layer, heads):
    T = q_ref.shape[1]
    W = q_ref.shape[2]
    K = HG_KDIM

    @pl.when(pl.program_id(1) == 0)
    def _():
        st_ref[...] = jnp.zeros_like(st_ref)

    lg = lbl_ref[...]
    ex = jnp.exp(lg - jnp.max(lg, axis=0, keepdims=True))
    sm = ex / jnp.sum(ex, axis=0, keepdims=True)
    lb = jnp.sum(sm[:layer + 1], axis=0, keepdims=True)

    q = q_ref[0]
    qf_ref[...] = q * jax.nn.sigmoid(q)
    f = lb + (1.0 - lb) * jax.nn.sigmoid(f_ref[0])
    kk_ref[...] = 1.0 - f
    logf = jnp.log(f)

    r = lax.broadcasted_iota(jnp.int32, (T, T), 0)
    c = lax.broadcasted_iota(jnp.int32, (T, T), 1)
    tri = jnp.where((c <= r) & (r // HG_CHUNK == c // HG_CHUNK), 1.0, 0.0).astype(BF16)
    hi = logf.astype(BF16)
    lo = (logf - hi.astype(F32)).astype(BF16)
    b_ref[...] = _dot(tri, hi) + _dot(tri, lo)

    pr = lax.broadcasted_iota(jnp.int32, (2 * K, 2 * K), 0)
    pc = lax.broadcasted_iota(jnp.int32, (2 * K, 2 * K), 1)
    seg_ones = jnp.where(pr // K == pc // K, 1.0, 0.0).astype(BF16)
    trow = lax.broadcasted_iota(jnp.int32, (HG_SUB, 1), 0)

    def diag_block(rb, carry):
        r0 = pl.multiple_of(rb * HG_SUB, HG_SUB)
        qb = qf_ref[pl.ds(r0, HG_SUB), :]
        bb = b_ref[pl.ds(r0, HG_SUB), :]
        for s in range(HG_SUB):
            ks = kk_ref[pl.ds(r0 + s, 1), :]
            bs = b_ref[pl.ds(r0 + s, 1), :]
            e = jnp.exp(jnp.minimum(bb - bs, 0.0)) * (qb * ks)
            e = jnp.where(trow >= s, e, 0.0)
            e_ref[s * HG_SUB:(s + 1) * HG_SUB, :] = e.astype(BF16)
        for p in range(W // (2 * K)):
            sl = slice(p * 2 * K, (p + 1) * 2 * K)
            w = _dot(e_ref[:, sl], seg_ones)
            part = jnp.zeros((HG_SUB, 2 * K), F32)
            for s in range(HG_SUB):
                vs = v_ref[0, pl.ds(r0 + s, 1), sl]
                part = part + w[s * HG_SUB:(s + 1) * HG_SUB, :] * vs
            o_ref[pl.ds(r0, HG_SUB), sl] = part
        return carry

    lax.fori_loop(0, T // HG_SUB, diag_block, 0)

    nsub = HG_CHUNK // HG_SUB

    def chunk(ci, carry):
        r0 = pl.multiple_of(ci * HG_CHUNK, HG_CHUNK)
        for h in range(heads):
            ls = slice(h * K, (h + 1) * K)
            bq = b_ref[pl.ds(r0, HG_CHUNK), ls]
            qh = qf_ref[pl.ds(r0, HG_CHUNK), ls]
            kh = kk_ref[pl.ds(r0, HG_CHUNK), ls]
            vh = v_ref[0, pl.ds(r0, HG_CHUNK), ls]
            vhb = vh.astype(BF16)
            st = st_ref[h]
            o_inter = _dot_nt((qh * jnp.exp(bq)).astype(BF16), st.astype(BF16))
            o_ref[pl.ds(r0, HG_CHUNK), ls] += o_inter
            for blk in range(1, nsub):
                lo_r = blk * HG_SUB
                beta = bq[lo_r - 1:lo_r, :]
                qt = qh[lo_r:lo_r + HG_SUB] * jnp.exp(bq[lo_r:lo_r + HG_SUB] - beta)
                kt = kh[:lo_r] * jnp.exp(beta - bq[:lo_r])
                sc = _dot_nt(qt.astype(BF16), kt.astype(BF16))
                o_ref[pl.ds(r0 + lo_r, HG_SUB), ls] += _dot(sc.astype(BF16), vhb[:lo_r])
            b_last = bq[HG_CHUNK - 1:HG_CHUNK, :]
            kd = kh * jnp.exp(b_last - bq)
            st_ref[h] = st * jnp.exp(b_last) + _dot(vh.T.astype(BF16), kd.astype(BF16))
        return carry

    lax.fori_loop(0, T // HG_CHUNK, chunk, 0)

    g = g_ref[0]
    sg = g * jax.nn.sigmoid(g)
    for h in range(heads):
        ls = slice(h * K, (h + 1) * K)
        y_ref[0, :, ls] = (_rms(o_ref[:, ls], gn_ref[:, ls]) * sg[:, ls]).astype(y_ref.dtype)


def _hgrn(proj3, lb_logits, gnorm, *, layer, width, t_pref=256):
    bsz, seq, _ = proj3.shape
    heads = width // HG_KDIM
    T = _tile(seq, t_pref)
    assert T % HG_CHUNK == 0 and width % (2 * HG_KDIM) == 0
    nl = lb_logits.shape[0]

    def col(k):
        return pl.BlockSpec((1, T, width), lambda b, t: (b, t, k))

    return pl.pallas_call(
        functools.partial(_hgrn_kernel, layer=layer, heads=heads),
        grid=(bsz, seq // T),
        in_specs=[col(0), col(1), col(2), col(3),
                  pl.BlockSpec((nl, width), lambda b, t: (0, 0)),
                  pl.BlockSpec((1, width), lambda b, t: (0, 0))],
        out_specs=pl.BlockSpec((1, T, width), lambda b, t: (b, t, 0)),
        out_shape=jax.ShapeDtypeStruct((bsz, seq, width), BF16),
        scratch_shapes=[
            pltpu.VMEM((heads, HG_KDIM, HG_KDIM), F32),
            pltpu.VMEM((T, width), F32),
            pltpu.VMEM((T, width), F32),
            pltpu.VMEM((T, width), F32),
            pltpu.VMEM((T, width), F32),
            pltpu.VMEM((HG_SUB * HG_SUB, width), BF16),
        ],
        compiler_params=_params("parallel", "arbitrary"),
        name="hgrn2",
    )(proj3, proj3, proj3, proj3, lb_logits, gnorm)


def _shift_rows(x, d, fill):
    rows = lax.broadcasted_iota(jnp.int32, (x.shape[0], 1), 0)
    return jnp.where(rows >= d, pltpu.roll(x, d, 0), fill)


def _rglru_kernel(x_ref, gate_ref, cw_ref, cb_ref, wa_ref, ba_ref, wx_ref, bx_ref, lam_ref, y_ref,
                  tail_ref, h_ref, *, blocks):
    T = x_ref.shape[1]
    W = x_ref.shape[2]
    bw = W // blocks
    t_idx = pl.program_id(1)

    @pl.when(t_idx == 0)
    def _():
        tail_ref[...] = jnp.zeros_like(tail_ref)
        h_ref[...] = jnp.zeros_like(h_ref)

    x = x_ref[0]
    ext = jnp.concatenate([tail_ref[...], x], axis=0)
    xc = cb_ref[...] + x * cw_ref[0:1, :]
    for j in range(1, CONV_WIDTH):
        xc = xc + pltpu.roll(ext, j, 0)[SUBLANES:, :] * cw_ref[j:j + 1, :]
    tail_ref[...] = x[T - SUBLANES:, :]

    xcb = xc.astype(BF16)
    ra, rx = [], []
    for n in range(blocks):
        blk = xcb[:, n * bw:(n + 1) * bw]
        ra.append(_dot(blk, wa_ref[n]))
        rx.append(_dot(blk, wx_ref[n]))
    rg = jax.nn.sigmoid(jnp.concatenate(ra, axis=-1) + ba_ref[...])
    ig = jax.nn.sigmoid(jnp.concatenate(rx, axis=-1) + bx_ref[...])

    lam = lam_ref[...]
    log_sig = -(jnp.maximum(-lam, 0.0) + jnp.log1p(jnp.exp(-jnp.abs(lam))))
    log_a = LRU_C * rg * log_sig
    a = jnp.exp(log_a)
    mult = jnp.sqrt(-jnp.tanh(log_a) * (a * a + 1.0))
    rows = lax.broadcasted_iota(jnp.int32, (T, 1), 0)
    mult = jnp.where((rows == 0) & (t_idx == 0), 1.0, mult)
    u = xc * ig * mult

    d = 1
    while d < T:
        u = a * _shift_rows(u, d, 0.0) + u
        a = a * _shift_rows(a, d, 1.0)
        d *= 2
    h = a * h_ref[...] + u
    h_ref[...] = h[T - 1:T, :]

    y_ref[0] = (h * jax.nn.gelu(gate_ref[0])).astype(y_ref.dtype)


def _rglru(proj3, conv_w, conv_b, wa, ba, wx, bx, lam, *, width, col0, t_pref=256):
    bsz, seq, _ = proj3.shape
    blocks = wa.shape[0]
    T = _tile(seq, t_pref)
    row = pl.BlockSpec((1, width), lambda b, t: (0, 0))
    wspec = pl.BlockSpec(wa.shape, lambda b, t: (0, 0, 0))
    return pl.pallas_call(
        functools.partial(_rglru_kernel, blocks=blocks),
        grid=(bsz, seq // T),
        in_specs=[
            pl.BlockSpec((1, T, width), lambda b, t: (b, t, col0)),
            pl.BlockSpec((1, T, width), lambda b, t: (b, t, col0 + 1)),
            pl.BlockSpec((CONV_WIDTH, width), lambda b, t: (0, 0)),
            row, wspec, row, wspec, row, row,
        ],
        out_specs=pl.BlockSpec((1, T, width), lambda b, t: (b, t, 0)),
        out_shape=jax.ShapeDtypeStruct((bsz, seq, width), BF16),
        scratch_shapes=[pltpu.VMEM((SUBLANES, width), F32), pltpu.VMEM((1, width), F32)],
        compiler_params=_params("parallel", "arbitrary"),
        name="rglru",
    )(proj3, proj3, conv_w, conv_b, wa, ba, wx, bx, lam)


def _merge_kernel(x_ref, ya_ref, yb_ref, ga_ref, gb_ref, bg_ref, wa_ref, wb_ref, wo_ref, o_ref):
    ga = jax.nn.sigmoid(ga_ref[...] + bg_ref[0:1, :])
    gb = jax.nn.sigmoid(gb_ref[...] + bg_ref[1:2, :])
    merged = ga * _dot(ya_ref[...], wa_ref[...]) + gb * _dot(yb_ref[...], wb_ref[...])
    o_ref[...] = x_ref[...] + _dot(merged.astype(BF16), wo_ref[...])


def _merge(x2, ya2, yb2, proj2, b_gate, w_a, w_b, w_o, *, gate_col0, tm_pref=256):
    m, d = x2.shape
    wa_w, wb_w = ya2.shape[1], yb2.shape[1]
    tm = _tile(m, tm_pref)
    return pl.pallas_call(
        _merge_kernel,
        grid=(m // tm,),
        in_specs=[
            pl.BlockSpec((tm, d), lambda i: (i, 0)),
            pl.BlockSpec((tm, wa_w), lambda i: (i, 0)),
            pl.BlockSpec((tm, wb_w), lambda i: (i, 0)),
            pl.BlockSpec((tm, d), lambda i: (i, gate_col0)),
            pl.BlockSpec((tm, d), lambda i: (i, gate_col0 + 1)),
            _resident(b_gate.shape), _resident(w_a.shape), _resident(w_b.shape), _resident(w_o.shape),
        ],
        out_specs=pl.BlockSpec((tm, d), lambda i: (i, 0)),
        out_shape=jax.ShapeDtypeStruct((m, d), F32),
        compiler_params=_params("parallel"),
        name="merge",
    )(x2, ya2, yb2, proj2, proj2, b_gate, w_a, w_b, w_o)


def _xattn_kernel(x_ref, g_ref, wq_ref, k_ref, v_ref, wo_ref, o_ref):
    x = x_ref[0]
    d = x.shape[-1]
    hd = d // XA_HEADS
    q = _dot(_rms(x, g_ref[...]).astype(BF16), wq_ref[...])
    outs = []
    for h in range(XA_HEADS):
        ls = slice(h * hd, (h + 1) * hd)
        s = _dot_nt(q[:, ls].astype(BF16), k_ref[0, :, ls]) * (hd ** -0.5)
        p = jnp.exp(s - jnp.max(s, axis=-1, keepdims=True))
        p = p / jnp.sum(p, axis=-1, keepdims=True)
        outs.append(_dot(p.astype(BF16), v_ref[0, :, ls]))
    o = jnp.concatenate(outs, axis=-1).astype(BF16)
    o_ref[0] = x + _dot(o, wo_ref[...])


def _xattn(x3, gain, wq, kv3, wo, *, tm_pref=256):
    bsz, seq, d = x3.shape
    nm = kv3.shape[1]
    tm = _tile(seq, tm_pref)
    return pl.pallas_call(
        _xattn_kernel,
        grid=(bsz, seq // tm),
        in_specs=[
            pl.BlockSpec((1, tm, d), lambda b, t: (b, t, 0)),
            _resident(gain.shape), _resident(wq.shape),
            pl.BlockSpec((1, nm, d), lambda b, t: (b, 0, 0)),
            pl.BlockSpec((1, nm, d), lambda b, t: (b, 0, 1)),
            _resident(wo.shape),
        ],
        out_specs=pl.BlockSpec((1, tm, d), lambda b, t: (b, t, 0)),
        out_shape=jax.ShapeDtypeStruct((bsz, seq, d), F32),
        compiler_params=_params("parallel", "parallel"),
        name="xattn",
    )(x3, gain, wq, kv3, kv3, wo)


def kernel(x, mem, ffn1_norm, ffn1_w_up, ffn1_w_down, mix_norm, w_in, b_gate, hgrn_lb_logits, hgrn_norm, conv_w, conv_b, lru_wa, lru_ba, lru_wx, lru_bx, lru_lambda, w_branch_a, w_branch_b, w_out, xattn_norm, mem_norm, xattn_wq, xattn_wkv, xattn_wo, ffn2_norm, ffn2_w_up, ffn2_w_down, final_norm):
    bsz, seq, d = x.shape
    nm = mem.shape[1]
    depth = ffn1_norm.shape[0]
    hg_w = hgrn_norm.shape[1]
    lru_w = lru_lambda.shape[1]
    assert hg_w == lru_w and 4 * hg_w + 2 * lru_w == 3 * d
    bf = lambda w: w.astype(BF16)
    row = lambda v: v.reshape(1, -1)
    fgain = row(final_norm)

    x2 = x.reshape(bsz * seq, d)
    mem2 = mem.reshape(bsz * nm, d)
    for l in range(depth):
        last = l == depth - 1
        x2 = _ffn(x2, row(ffn1_norm[l]), bf(ffn1_w_up[l]), bf(ffn1_w_down[l]), fgain, final_norm=False)

        proj2 = _norm_proj(x2, row(mix_norm[l]), bf(w_in[l]), F32, name="in_proj")
        proj3 = proj2.reshape(bsz, seq, -1)
        y_a = _hgrn(proj3, hgrn_lb_logits, row(hgrn_norm[l]), layer=l, width=hg_w)
        y_b = _rglru(proj3, conv_w[l], row(conv_b[l]), bf(lru_wa[l]), row(lru_ba[l]),
                     bf(lru_wx[l]), row(lru_bx[l]), row(lru_lambda[l]), width=lru_w, col0=4)
        x2 = _merge(x2, y_a.reshape(bsz * seq, hg_w), y_b.reshape(bsz * seq, lru_w), proj2,
                    b_gate[l], bf(w_branch_a[l]), bf(w_branch_b[l]), bf(w_out[l]), gate_col0=3)

        kv2 = _norm_proj(mem2, row(mem_norm[l]), bf(xattn_wkv[l]), BF16, name="kv_proj")
        x3 = _xattn(x2.reshape(bsz, seq, d), row(xattn_norm[l]), bf(xattn_wq[l]),
                    kv2.reshape(bsz, nm, 2 * d), bf(xattn_wo[l]))
        x2 = x3.reshape(bsz * seq, d)

        x2 = _ffn(x2, row(ffn2_norm[l]), bf(ffn2_w_up[l]), bf(ffn2_w_down[l]), fgain,
                  final_norm=last)
    return x2.reshape(bsz, seq, d)
```

```python
import functools

import jax
import jax.numpy as jnp
from jax import lax
from jax.experimental import pallas as pl
from jax.experimental.pallas import tpu as pltpu

EPS = 1e-6
HG_KDIM = 128
XA_HEADS = 4
LRU_C = 8.0
CONV_WIDTH = 4
LOG2E = 1.4426950408889634
LANES = 128
SUBLANES = 8
V7X_VMEM_BYTES = 64 * 1024 * 1024
VMEM_LIMIT = V7X_VMEM_BYTES - 8 * 1024 * 1024

HG_CHUNK = 64
HG_SUB = 16
HG_PAR = 4

F32 = jnp.float32
BF16 = jnp.bfloat16


def _params(*semantics):
    return pltpu.CompilerParams(dimension_semantics=semantics, vmem_limit_bytes=VMEM_LIMIT)


def _tile(n, pref):
    t = min(n, pref)
    while n % t:
        t //= 2
    return t


def _rms(xf, g):
    ms = jnp.mean(xf * xf, axis=-1, keepdims=True)
    return xf * lax.rsqrt(ms + EPS) * g


def _dot(a, b):
    return jnp.dot(a, b, preferred_element_type=F32)


def _dot_nt(a, b):
    return lax.dot_general(a, b, (((1,), (1,)), ((), ())), preferred_element_type=F32)


def _dot_tn(a, b):
    return lax.dot_general(a, b, (((0,), (0,)), ((), ())), preferred_element_type=F32)


def _resident(shape):
    return pl.BlockSpec(shape, lambda *_: (0,) * len(shape), pipeline_mode=pl.Buffered(1))


def _ffn_kernel(x_ref, g_ref, wg_ref, wu_ref, wd_ref, fg_ref, o_ref, hn_ref, *, final_norm):
    j = pl.program_id(1)

    @pl.when(j == 0)
    def _():
        x = x_ref[...]
        hn_ref[...] = _rms(x, g_ref[...]).astype(BF16)
        o_ref[...] = x

    h = hn_ref[...]
    gate = _dot(h, wg_ref[...])
    up = _dot(h, wu_ref[...])
    act = (0.5 * (gate * jax.nn.sigmoid(gate) * up)).astype(BF16)
    o_ref[...] += _dot(act, wd_ref[...])

    if final_norm:
        @pl.when(j == pl.num_programs(1) - 1)
        def _():
            o_ref[...] = _rms(o_ref[...], fg_ref[...])


def _ffn(x2, gain, w_up, w_down, final_gain, *, final_norm, tm_pref=1024, tf_pref=512):
    m, d = x2.shape
    dff = w_down.shape[0]
    tm = _tile(m, tm_pref)
    tf = _tile(dff, tf_pref)
    nf = dff // tf
    return pl.pallas_call(
        functools.partial(_ffn_kernel, final_norm=final_norm),
        grid=(m // tm, nf),
        in_specs=[
            pl.BlockSpec((tm, d), lambda i, j: (i, 0)),
            pl.BlockSpec((1, d), lambda i, j: (0, 0)),
            pl.BlockSpec((d, tf), lambda i, j: (0, j)),
            pl.BlockSpec((d, tf), lambda i, j: (0, j + nf)),
            pl.BlockSpec((tf, d), lambda i, j: (j, 0)),
            pl.BlockSpec((1, d), lambda i, j: (0, 0)),
        ],
        out_specs=pl.BlockSpec((tm, d), lambda i, j: (i, 0)),
        out_shape=jax.ShapeDtypeStruct((m, d), F32),
        scratch_shapes=[pltpu.VMEM((tm, d), BF16)],
        compiler_params=_params("parallel", "arbitrary"),
        name="ffn_final" if final_norm else "ffn",
    )(x2, gain, w_up, w_up, w_down, final_gain)


def _norm_proj_kernel(x_ref, g_ref, w_ref, o_ref, hn_ref):
    @pl.when(pl.program_id(1) == 0)
    def _():
        hn_ref[...] = _rms(x_ref[...], g_ref[...]).astype(BF16)

    o_ref[...] = _dot(hn_ref[...], w_ref[...]).astype(o_ref.dtype)


def _norm_proj(x2, gain, w, out_dtype, *, name, tm_pref=1024, tn_pref=1024):
    m, d = x2.shape
    n = w.shape[1]
    tm = _tile(m, tm_pref)
    tn = _tile(n, tn_pref)
    return pl.pallas_call(
        _norm_proj_kernel,
        grid=(m // tm, n // tn),
        in_specs=[
            pl.BlockSpec((tm, d), lambda i, j: (i, 0)),
            pl.BlockSpec((1, d), lambda i, j: (0, 0)),
            pl.BlockSpec((d, tn), lambda i, j: (0, j)),
        ],
        out_specs=pl.BlockSpec((tm, tn), lambda i, j: (i, j)),
        out_shape=jax.ShapeDtypeStruct((m, n), out_dtype),
        scratch_shapes=[pltpu.VMEM((tm, d), BF16)],
        compiler_params=_params("parallel", "arbitrary"),
        name=name,
    )(x2, gain, w)


def _silu(x):
    h = 0.5 * x
    return h + h * jnp.tanh(h)


def _hgrn_kernel(q_ref, f_ref, v_ref, g_ref, lbl_ref, gn_ref, y_ref,
                 st_ref, qf_ref, kk_ref, vv_ref, b_ref, od_ref, oc_ref, es_ref, *, layer, heads):
    T = q_ref.shape[1]
    K = HG_KDIM
    H8 = SUBLANES

    @pl.when(pl.program_id(1) == 0)
    def _():
        st_ref[...] = jnp.zeros_like(st_ref)

    lg = lbl_ref[...]
    ex = jnp.exp(lg - jnp.max(lg, axis=0, keepdims=True))
    sm = ex / jnp.sum(ex, axis=0, keepdims=True)
    lb = jnp.sum(sm[:layer + 1], axis=0, keepdims=True)

    r = lax.broadcasted_iota(jnp.int32, (T, T), 0)
    c = lax.broadcasted_iota(jnp.int32, (T, T), 1)
    tri = jnp.where((c <= r) & (r // HG_CHUNK == c // HG_CHUNK), 1.0, 0.0).astype(BF16)
    f_mid = 0.5 * (1.0 + lb)
    f_half = 0.5 * (1.0 - lb)
    for h in range(heads):
        ls = slice(h * K, (h + 1) * K)
        qf_ref[h] = _silu(q_ref[0, :, ls])
        swing = f_half[:, ls] * jnp.tanh(0.5 * f_ref[0, :, ls])
        f = f_mid[:, ls] + swing
        kk_ref[h] = f_half[:, ls] - swing
        vv_ref[h] = v_ref[0, :, ls]
        logf = jnp.log(f) * LOG2E
        hi = logf.astype(BF16)
        lo = (logf - hi.astype(F32)).astype(BF16)
        b_ref[h] = _dot(tri, hi) + _dot(tri, lo)

    pr = lax.broadcasted_iota(jnp.int32, (2 * K, 2 * K), 0)
    pc = lax.broadcasted_iota(jnp.int32, (2 * K, 2 * K), 1)
    seg_ones = jnp.where(pr // K == pc // K, 1.0, 0.0).astype(BF16)
    trow = lax.broadcasted_iota(jnp.int32, (H8, 1), 0)
    zeros8 = jnp.zeros((H8, K), F32)

    def diag_block(r0, e_ref):
        for p in range(heads // 2):
            for hh in range(2):
                h = 2 * p + hh
                q_lo = qf_ref[h, pl.ds(r0, H8), :]
                q_hi = qf_ref[h, pl.ds(r0 + H8, H8), :]
                b_lo = b_ref[h, pl.ds(r0, H8), :]
                b_hi = b_ref[h, pl.ds(r0 + H8, H8), :]
                for s in range(HG_SUB):
                    ks = kk_ref[h, pl.ds(r0 + s, 1), :]
                    bs = b_ref[h, pl.ds(r0 + s, 1), :]
                    e_hi = jnp.exp2(b_hi - bs) * (q_hi * ks)
                    if s == 0:
                        e_lo = jnp.exp2(b_lo - bs) * (q_lo * ks)
                    elif s < H8:
                        e_lo = jnp.where(trow >= s, jnp.exp2(b_lo - bs) * (q_lo * ks), 0.0)
                    else:
                        e_lo = zeros8
                        if s > H8:
                            e_hi = jnp.where(trow >= s - H8, e_hi, 0.0)
                    e_ref[p, s * HG_SUB:(s + 1) * HG_SUB, hh * K:(hh + 1) * K] = (
                        jnp.concatenate([e_lo, e_hi], axis=0).astype(BF16))
            w = _dot(e_ref[p], seg_ones)
            for hh in range(2):
                h = 2 * p + hh
                o_lo = jnp.zeros((H8, K), F32)
                o_hi = jnp.zeros((H8, K), F32)
                for s in range(HG_SUB):
                    vs = vv_ref[h, pl.ds(r0 + s, 1), :]
                    if s < H8:
                        o_lo = o_lo + w[s * HG_SUB:s * HG_SUB + H8, hh * K:(hh + 1) * K] * vs
                    o_hi = o_hi + w[s * HG_SUB + H8:(s + 1) * HG_SUB, hh * K:(hh + 1) * K] * vs
                od_ref[h, pl.ds(r0, H8), :] = o_lo
                od_ref[h, pl.ds(r0 + H8, H8), :] = o_hi

    n_par = es_ref.shape[0]

    def diag_group(gi, carry):
        for u in range(n_par):
            diag_block(pl.multiple_of((gi * n_par + u) * HG_SUB, HG_SUB), es_ref.at[u])
        return carry

    lax.fori_loop(0, T // (HG_SUB * n_par), diag_group, 0)

    nsub = HG_CHUNK // HG_SUB

    states = [st_ref[h] for h in range(heads)]
    for ci in range(T // HG_CHUNK):
        r0 = ci * HG_CHUNK
        o_inter, scores, vbs = [], [], []
        for h in range(heads):
            bq = b_ref[h, pl.ds(r0, HG_CHUNK), :]
            qh = qf_ref[h, pl.ds(r0, HG_CHUNK), :]
            kh = kk_ref[h, pl.ds(r0, HG_CHUNK), :]
            vhb = vv_ref[h, pl.ds(r0, HG_CHUNK), :].astype(BF16)
            st = states[h]
            o_inter.append(_dot_nt((qh * jnp.exp2(bq)).astype(BF16), st.astype(BF16)))
            sc_h = []
            for blk in range(1, nsub):
                lo_r = blk * HG_SUB
                beta = bq[lo_r - 1:lo_r, :]
                qt = qh[lo_r:lo_r + HG_SUB] * jnp.exp2(bq[lo_r:lo_r + HG_SUB] - beta)
                kt = kh[:lo_r] * jnp.exp2(beta - bq[:lo_r])
                sc_h.append(_dot_nt(qt.astype(BF16), kt.astype(BF16)).astype(BF16))
            scores.append(sc_h)
            vbs.append(vhb)
            b_last = bq[HG_CHUNK - 1:HG_CHUNK, :]
            kd = (kh * jnp.exp2(b_last - bq)).astype(BF16)
            states[h] = st * jnp.exp2(b_last) + _dot_tn(vhb, kd)
        for h in range(heads):
            offs = [jnp.zeros((HG_SUB, K), F32)]
            for blk in range(1, nsub):
                offs.append(_dot(scores[h][blk - 1], vbs[h][:blk * HG_SUB]))
            oc_ref[h, pl.ds(r0, HG_CHUNK), :] = o_inter[h] + jnp.concatenate(offs, axis=0)
    for h in range(heads):
        st_ref[h] = states[h]

    for h in range(heads):
        ls = slice(h * K, (h + 1) * K)
        g = g_ref[0, :, ls]
        o = od_ref[h] + oc_ref[h]
        y_ref[0, :, ls] = (_rms(o, gn_ref[:, ls]) * _silu(g)).astype(y_ref.dtype)


def _hgrn(proj3, lb_logits, gnorm, *, layer, width, t_pref=256):
    bsz, seq, _ = proj3.shape
    heads = width // HG_KDIM
    T = _tile(seq, t_pref)
    assert T % HG_CHUNK == 0 and T % (HG_SUB * HG_PAR) == 0 and width % (2 * HG_KDIM) == 0
    nl = lb_logits.shape[0]

    def col(k):
        return pl.BlockSpec((1, T, width), lambda b, t: (b, t, k))

    return pl.pallas_call(
        functools.partial(_hgrn_kernel, layer=layer, heads=heads),
        grid=(bsz, seq // T),
        in_specs=[col(0), col(1), col(2), col(3),
                  pl.BlockSpec((nl, width), lambda b, t: (0, 0)),
                  pl.BlockSpec((1, width), lambda b, t: (0, 0))],
        out_specs=pl.BlockSpec((1, T, width), lambda b, t: (b, t, 0)),
        out_shape=jax.ShapeDtypeStruct((bsz, seq, width), BF16),
        scratch_shapes=[
            pltpu.VMEM((heads, HG_KDIM, HG_KDIM), F32),
            pltpu.VMEM((heads, T, HG_KDIM), F32),
            pltpu.VMEM((heads, T, HG_KDIM), F32),
            pltpu.VMEM((heads, T, HG_KDIM), F32),
            pltpu.VMEM((heads, T, HG_KDIM), F32),
            pltpu.VMEM((heads, T, HG_KDIM), F32),
            pltpu.VMEM((heads, T, HG_KDIM), F32),
            pltpu.VMEM((HG_PAR, heads // 2, HG_SUB * HG_SUB, 2 * HG_KDIM), BF16),
        ],
        compiler_params=_params("parallel", "arbitrary"),
        name="hgrn2",
    )(proj3, proj3, proj3, proj3, lb_logits, gnorm)


def _shift_rows(x, d, fill):
    rows = lax.broadcasted_iota(jnp.int32, (x.shape[0], 1), 0)
    return jnp.where(rows >= d, pltpu.roll(x, d, 0), fill)


def _rglru_kernel(x_ref, gate_ref, cw_ref, cb_ref, wa_ref, ba_ref, wx_ref, bx_ref, lam_ref, y_ref,
                  tail_ref, h_ref, *, blocks):
    T = x_ref.shape[1]
    W = x_ref.shape[2]
    bw = W // blocks
    t_idx = pl.program_id(1)

    @pl.when(t_idx == 0)
    def _():
        tail_ref[...] = jnp.zeros_like(tail_ref)
        h_ref[...] = jnp.zeros_like(h_ref)

    x = x_ref[0]
    ext = jnp.concatenate([tail_ref[...], x], axis=0)
    xc = cb_ref[...] + x * cw_ref[0:1, :]
    for j in range(1, CONV_WIDTH):
        xc = xc + pltpu.roll(ext, j, 0)[SUBLANES:, :] * cw_ref[j:j + 1, :]
    tail_ref[...] = x[T - SUBLANES:, :]

    xcb = xc.astype(BF16)
    ra, rx = [], []
    for n in range(blocks):
        blk = xcb[:, n * bw:(n + 1) * bw]
        ra.append(_dot(blk, wa_ref[n]))
        rx.append(_dot(blk, wx_ref[n]))
    rg = jax.nn.sigmoid(jnp.concatenate(ra, axis=-1) + ba_ref[...])
    ig = jax.nn.sigmoid(jnp.concatenate(rx, axis=-1) + bx_ref[...])

    lam = lam_ref[...]
    log_sig = -(jnp.maximum(-lam, 0.0) + jnp.log1p(jnp.exp(-jnp.abs(lam))))
    log_a = LRU_C * rg * log_sig
    a = jnp.exp(log_a)
    mult = jnp.sqrt(-jnp.tanh(log_a) * (a * a + 1.0))
    rows = lax.broadcasted_iota(jnp.int32, (T, 1), 0)
    mult = jnp.where((rows == 0) & (t_idx == 0), 1.0, mult)
    u = xc * ig * mult

    d = 1
    while d < T:
        u = a * _shift_rows(u, d, 0.0) + u
        a = a * _shift_rows(a, d, 1.0)
        d *= 2
    h = a * h_ref[...] + u
    h_ref[...] = h[T - 1:T, :]

    y_ref[0] = (h * jax.nn.gelu(gate_ref[0])).astype(y_ref.dtype)


def _rglru(proj3, conv_w, conv_b, wa, ba, wx, bx, lam, *, width, col0, t_pref=256):
    bsz, seq, _ = proj3.shape
    blocks = wa.shape[0]
    T = _tile(seq, t_pref)
    row = pl.BlockSpec((1, width), lambda b, t: (0, 0))
    wspec = pl.BlockSpec(wa.shape, lambda b, t: (0, 0, 0))
    return pl.pallas_call(
        functools.partial(_rglru_kernel, blocks=blocks),
        grid=(bsz, seq // T),
        in_specs=[
            pl.BlockSpec((1, T, width), lambda b, t: (b, t, col0)),
            pl.BlockSpec((1, T, width), lambda b, t: (b, t, col0 + 1)),
            pl.BlockSpec((CONV_WIDTH, width), lambda b, t: (0, 0)),
            row, wspec, row, wspec, row, row,
        ],
        out_specs=pl.BlockSpec((1, T, width), lambda b, t: (b, t, 0)),
        out_shape=jax.ShapeDtypeStruct((bsz, seq, width), BF16),
        scratch_shapes=[pltpu.VMEM((SUBLANES, width), F32), pltpu.VMEM((1, width), F32)],
        compiler_params=_params("parallel", "arbitrary"),
        name="rglru",
    )(proj3, proj3, conv_w, conv_b, wa, ba, wx, bx, lam)


def _merge_kernel(x_ref, ya_ref, yb_ref, ga_ref, gb_ref, bg_ref, wa_ref, wb_ref, wo_ref, o_ref):
    ga = jax.nn.sigmoid(ga_ref[...] + bg_ref[0:1, :])
    gb = jax.nn.sigmoid(gb_ref[...] + bg_ref[1:2, :])
    merged = ga * _dot(ya_ref[...], wa_ref[...]) + gb * _dot(yb_ref[...], wb_ref[...])
    o_ref[...] = x_ref[...] + _dot(merged.astype(BF16), wo_ref[...])


def _merge(x2, ya2, yb2, proj2, b_gate, w_a, w_b, w_o, *, gate_col0, tm_pref=256):
    m, d = x2.shape
    wa_w, wb_w = ya2.shape[1], yb2.shape[1]
    tm = _tile(m, tm_pref)
    return pl.pallas_call(
        _merge_kernel,
        grid=(m // tm,),
        in_specs=[
            pl.BlockSpec((tm, d), lambda i: (i, 0)),
            pl.BlockSpec((tm, wa_w), lambda i: (i, 0)),
            pl.BlockSpec((tm, wb_w), lambda i: (i, 0)),
            pl.BlockSpec((tm, d), lambda i: (i, gate_col0)),
            pl.BlockSpec((tm, d), lambda i: (i, gate_col0 + 1)),
            _resident(b_gate.shape), _resident(w_a.shape), _resident(w_b.shape), _resident(w_o.shape),
        ],
        out_specs=pl.BlockSpec((tm, d), lambda i: (i, 0)),
        out_shape=jax.ShapeDtypeStruct((m, d), F32),
        compiler_params=_params("parallel"),
        name="merge",
    )(x2, ya2, yb2, proj2, proj2, b_gate, w_a, w_b, w_o)


def _xattn_kernel(x_ref, g_ref, wq_ref, k_ref, v_ref, wo_ref, o_ref):
    x = x_ref[0]
    d = x.shape[-1]
    hd = d // XA_HEADS
    q = _dot(_rms(x, g_ref[...]).astype(BF16), wq_ref[...])
    outs = []
    for h in range(XA_HEADS):
        ls = slice(h * hd, (h + 1) * hd)
        s = _dot_nt(q[:, ls].astype(BF16), k_ref[0, :, ls]) * (hd ** -0.5)
        p = jnp.exp(s - jnp.max(s, axis=-1, keepdims=True))
        p = p / jnp.sum(p, axis=-1, keepdims=True)
        outs.append(_dot(p.astype(BF16), v_ref[0, :, ls]))
    o = jnp.concatenate(outs, axis=-1).astype(BF16)
    o_ref[0] = x + _dot(o, wo_ref[...])


def _xattn(x3, gain, wq, kv3, wo, *, tm_pref=256):
    bsz, seq, d = x3.shape
    nm = kv3.shape[1]
    tm = _tile(seq, tm_pref)
    return pl.pallas_call(
        _xattn_kernel,
        grid=(bsz, seq // tm),
        in_specs=[
            pl.BlockSpec((1, tm, d), lambda b, t: (b, t, 0)),
            _resident(gain.shape), _resident(wq.shape),
            pl.BlockSpec((1, nm, d), lambda b, t: (b, 0, 0)),
            pl.BlockSpec((1, nm, d), lambda b, t: (b, 0, 1)),
            _resident(wo.shape),
        ],
        out_specs=pl.BlockSpec((1, tm, d), lambda b, t: (b, t, 0)),
        out_shape=jax.ShapeDtypeStruct((bsz, seq, d), F32),
        compiler_params=_params("parallel", "parallel"),
        name="xattn",
    )(x3, gain, wq, kv3, kv3, wo)


def kernel(x, mem, ffn1_norm, ffn1_w_up, ffn1_w_down, mix_norm, w_in, b_gate, hgrn_lb_logits, hgrn_norm, conv_w, conv_b, lru_wa, lru_ba, lru_wx, lru_bx, lru_lambda, w_branch_a, w_branch_b, w_out, xattn_norm, mem_norm, xattn_wq, xattn_wkv, xattn_wo, ffn2_norm, ffn2_w_up, ffn2_w_down, final_norm):
    bsz, seq, d = x.shape
    nm = mem.shape[1]
    depth = ffn1_norm.shape[0]
    hg_w = hgrn_norm.shape[1]
    lru_w = lru_lambda.shape[1]
    assert hg_w == lru_w and 4 * hg_w + 2 * lru_w == 3 * d
    bf = lambda w: w.astype(BF16)
    row = lambda v: v.reshape(1, -1)
    fgain = row(final_norm)

    x2 = x.reshape(bsz * seq, d)
    mem2 = mem.reshape(bsz * nm, d)
    for l in range(depth):
        last = l == depth - 1
        x2 = _ffn(x2, row(ffn1_norm[l]), bf(ffn1_w_up[l]), bf(ffn1_w_down[l]), fgain, final_norm=False)

        proj2 = _norm_proj(x2, row(mix_norm[l]), bf(w_in[l]), F32, name="in_proj")
        proj3 = proj2.reshape(bsz, seq, -1)
        y_a = _hgrn(proj3, hgrn_lb_logits, row(hgrn_norm[l]), layer=l, width=hg_w)
        y_b = _rglru(proj3, conv_w[l], row(conv_b[l]), bf(lru_wa[l]), row(lru_ba[l]),
                     bf(lru_wx[l]), row(lru_bx[l]), row(lru_lambda[l]), width=lru_w, col0=4)
        x2 = _merge(x2, y_a.reshape(bsz * seq, hg_w), y_b.reshape(bsz * seq, lru_w), proj2,
                    b_gate[l], bf(w_branch_a[l]), bf(w_branch_b[l]), bf(w_out[l]), gate_col0=3)

        kv2 = _norm_proj(mem2, row(mem_norm[l]), bf(xattn_wkv[l]), BF16, name="kv_proj")
        x3 = _xattn(x2.reshape(bsz, seq, d), row(xattn_norm[l]), bf(xattn_wq[l]),
                    kv2.reshape(bsz, nm, 2 * d), bf(xattn_wo[l]))
        x2 = x3.reshape(bsz * seq, d)

        x2 = _ffn(x2, row(ffn2_norm[l]), bf(ffn2_w_up[l]), bf(ffn2_w_down[l]), fgain,
                  final_norm=last)
    return x2.reshape(bsz, seq, d)
```

```python
import functools

import jax
import jax.numpy as jnp
from jax import lax
from jax.experimental import pallas as pl
from jax.experimental.pallas import tpu as pltpu

EPS = 1e-6
HG_KDIM = 128
XA_HEADS = 4
LRU_C = 8.0
CONV_WIDTH = 4
LOG2E = 1.4426950408889634
LANES = 128
SUBLANES = 8
V7X_VMEM_BYTES = 64 * 1024 * 1024
VMEM_LIMIT = V7X_VMEM_BYTES - 8 * 1024 * 1024

HG_CHUNK = 64
HG_SUB = 16
HG_PAR = 4

F32 = jnp.float32
BF16 = jnp.bfloat16


def _params(*semantics):
    return pltpu.CompilerParams(dimension_semantics=semantics, vmem_limit_bytes=VMEM_LIMIT)


def _tile(n, pref):
    t = min(n, pref)
    while n % t:
        t //= 2
    return t


def _rms(xf, g):
    ms = jnp.mean(xf * xf, axis=-1, keepdims=True)
    return xf * lax.rsqrt(ms + EPS) * g


def _dot(a, b):
    return jnp.dot(a, b, preferred_element_type=F32)


def _dot_nt(a, b):
    return lax.dot_general(a, b, (((1,), (1,)), ((), ())), preferred_element_type=F32)


def _dot_tn(a, b):
    return lax.dot_general(a, b, (((0,), (0,)), ((), ())), preferred_element_type=F32)


def _resident(shape):
    return pl.BlockSpec(shape, lambda *_: (0,) * len(shape), pipeline_mode=pl.Buffered(1))


def _ffn_kernel(x_ref, g_ref, wg_ref, wu_ref, wd_ref, fg_ref, o_ref, hn_ref, *, final_norm):
    j = pl.program_id(1)

    @pl.when(j == 0)
    def _():
        x = x_ref[...]
        hn_ref[...] = _rms(x, g_ref[...]).astype(BF16)
        o_ref[...] = x

    h = hn_ref[...]
    gate = _dot(h, wg_ref[...])
    up = _dot(h, wu_ref[...])
    act = (0.5 * (gate * jax.nn.sigmoid(gate) * up)).astype(BF16)
    o_ref[...] += _dot(act, wd_ref[...])

    if final_norm:
        @pl.when(j == pl.num_programs(1) - 1)
        def _():
            o_ref[...] = _rms(o_ref[...], fg_ref[...])


def _ffn(x2, gain, w_up, w_down, final_gain, *, final_norm, tm_pref=1024, tf_pref=512):
    m, d = x2.shape
    dff = w_down.shape[0]
    tm = _tile(m, tm_pref)
    tf = _tile(dff, tf_pref)
    nf = dff // tf
    return pl.pallas_call(
        functools.partial(_ffn_kernel, final_norm=final_norm),
        grid=(m // tm, nf),
        in_specs=[
            pl.BlockSpec((tm, d), lambda i, j: (i, 0)),
            pl.BlockSpec((1, d), lambda i, j: (0, 0)),
            pl.BlockSpec((d, tf), lambda i, j: (0, j)),
            pl.BlockSpec((d, tf), lambda i, j: (0, j + nf)),
            pl.BlockSpec((tf, d), lambda i, j: (j, 0)),
            pl.BlockSpec((1, d), lambda i, j: (0, 0)),
        ],
        out_specs=pl.BlockSpec((tm, d), lambda i, j: (i, 0)),
        out_shape=jax.ShapeDtypeStruct((m, d), F32),
        scratch_shapes=[pltpu.VMEM((tm, d), BF16)],
        compiler_params=_params("parallel", "arbitrary"),
        name="ffn_final" if final_norm else "ffn",
    )(x2, gain, w_up, w_up, w_down, final_gain)


def _norm_proj_kernel(x_ref, g_ref, w_ref, o_ref, hn_ref):
    @pl.when(pl.program_id(1) == 0)
    def _():
        hn_ref[...] = _rms(x_ref[...], g_ref[...]).astype(BF16)

    o_ref[...] = _dot(hn_ref[...], w_ref[...]).astype(o_ref.dtype)


def _norm_proj(x2, gain, w, out_dtype, *, name, tm_pref=1024, tn_pref=1024):
    m, d = x2.shape
    n = w.shape[1]
    tm = _tile(m, tm_pref)
    tn = _tile(n, tn_pref)
    return pl.pallas_call(
        _norm_proj_kernel,
        grid=(m // tm, n // tn),
        in_specs=[
            pl.BlockSpec((tm, d), lambda i, j: (i, 0)),
            pl.BlockSpec((1, d), lambda i, j: (0, 0)),
            pl.BlockSpec((d, tn), lambda i, j: (0, j)),
        ],
        out_specs=pl.BlockSpec((tm, tn), lambda i, j: (i, j)),
        out_shape=jax.ShapeDtypeStruct((m, n), out_dtype),
        scratch_shapes=[pltpu.VMEM((tm, d), BF16)],
        compiler_params=_params("parallel", "arbitrary"),
        name=name,
    )(x2, gain, w)


def _logistic(x):
    return 0.5 * jnp.tanh(0.5 * x) + 0.5


def _silu(x):
    h = 0.5 * x
    return h + h * jnp.tanh(h)


def _hgrn_kernel(q_ref, f_ref, v_ref, g_ref, lbl_ref, gn_ref, y_ref,
                 st_ref, qf_ref, kk_ref, vv_ref, b_ref, od_ref, oc_ref, es_ref, *, layer, heads):
    T = q_ref.shape[1]
    K = HG_KDIM
    H8 = SUBLANES

    @pl.when(pl.program_id(1) == 0)
    def _():
        st_ref[...] = jnp.zeros_like(st_ref)

    lg = lbl_ref[...]
    ex = jnp.exp(lg - jnp.max(lg, axis=0, keepdims=True))
    sm = ex / jnp.sum(ex, axis=0, keepdims=True)
    lb = jnp.sum(sm[:layer + 1], axis=0, keepdims=True)

    r = lax.broadcasted_iota(jnp.int32, (T, T), 0)
    c = lax.broadcasted_iota(jnp.int32, (T, T), 1)
    tri = jnp.where((c <= r) & (r // HG_CHUNK == c // HG_CHUNK), 1.0, 0.0).astype(BF16)
    f_mid = 0.5 * (1.0 + lb)
    f_half = 0.5 * (1.0 - lb)
    for h in range(heads):
        ls = slice(h * K, (h + 1) * K)
        qf_ref[h] = _silu(q_ref[0, :, ls])
        swing = f_half[:, ls] * jnp.tanh(0.5 * f_ref[0, :, ls])
        f = f_mid[:, ls] + swing
        kk_ref[h] = f_half[:, ls] - swing
        vv_ref[h] = v_ref[0, :, ls]
        logf = jnp.log(f) * LOG2E
        hi = logf.astype(BF16)
        lo = (logf - hi.astype(F32)).astype(BF16)
        b_ref[h] = _dot(tri, hi) + _dot(tri, lo)

    pr = lax.broadcasted_iota(jnp.int32, (2 * K, 2 * K), 0)
    pc = lax.broadcasted_iota(jnp.int32, (2 * K, 2 * K), 1)
    seg_ones = jnp.where(pr // K == pc // K, 1.0, 0.0).astype(BF16)
    trow = lax.broadcasted_iota(jnp.int32, (H8, 1), 0)
    zeros8 = jnp.zeros((H8, K), F32)

    def diag_block(r0, e_ref):
        for p in range(heads // 2):
            for hh in range(2):
                h = 2 * p + hh
                q_lo = qf_ref[h, pl.ds(r0, H8), :]
                q_hi = qf_ref[h, pl.ds(r0 + H8, H8), :]
                b_lo = b_ref[h, pl.ds(r0, H8), :]
                b_hi = b_ref[h, pl.ds(r0 + H8, H8), :]
                for s in range(HG_SUB):
                    ks = kk_ref[h, pl.ds(r0 + s, 1), :]
                    bs = b_ref[h, pl.ds(r0 + s, 1), :]
                    e_hi = jnp.exp2(b_hi - bs) * (q_hi * ks)
                    if s == 0:
                        e_lo = jnp.exp2(b_lo - bs) * (q_lo * ks)
                    elif s < H8:
                        e_lo = jnp.where(trow >= s, jnp.exp2(b_lo - bs) * (q_lo * ks), 0.0)
                    else:
                        e_lo = zeros8
                        if s > H8:
                            e_hi = jnp.where(trow >= s - H8, e_hi, 0.0)
                    e_ref[p, s * HG_SUB:(s + 1) * HG_SUB, hh * K:(hh + 1) * K] = (
                        jnp.concatenate([e_lo, e_hi], axis=0).astype(BF16))
            w = _dot(e_ref[p], seg_ones)
            for hh in range(2):
                h = 2 * p + hh
                o_lo = jnp.zeros((H8, K), F32)
                o_hi = jnp.zeros((H8, K), F32)
                for s in range(HG_SUB):
                    vs = vv_ref[h, pl.ds(r0 + s, 1), :]
                    if s < H8:
                        o_lo = o_lo + w[s * HG_SUB:s * HG_SUB + H8, hh * K:(hh + 1) * K] * vs
                    o_hi = o_hi + w[s * HG_SUB + H8:(s + 1) * HG_SUB, hh * K:(hh + 1) * K] * vs
                od_ref[h, pl.ds(r0, H8), :] = o_lo
                od_ref[h, pl.ds(r0 + H8, H8), :] = o_hi

    n_par = es_ref.shape[0]

    def diag_group(gi, carry):
        for u in range(n_par):
            diag_block(pl.multiple_of((gi * n_par + u) * HG_SUB, HG_SUB), es_ref.at[u])
        return carry

    lax.fori_loop(0, T // (HG_SUB * n_par), diag_group, 0)

    nsub = HG_CHUNK // HG_SUB

    states = [st_ref[h] for h in range(heads)]
    for ci in range(T // HG_CHUNK):
        r0 = ci * HG_CHUNK
        o_inter, scores, vbs = [], [], []
        for h in range(heads):
            bq = b_ref[h, pl.ds(r0, HG_CHUNK), :]
            qh = qf_ref[h, pl.ds(r0, HG_CHUNK), :]
            kh = kk_ref[h, pl.ds(r0, HG_CHUNK), :]
            vhb = vv_ref[h, pl.ds(r0, HG_CHUNK), :].astype(BF16)
            st = states[h]
            o_inter.append(_dot_nt((qh * jnp.exp2(bq)).astype(BF16), st.astype(BF16)))
            sc_h = []
            for blk in range(1, nsub):
                lo_r = blk * HG_SUB
                beta = bq[lo_r - 1:lo_r, :]
                qt = qh[lo_r:lo_r + HG_SUB] * jnp.exp2(bq[lo_r:lo_r + HG_SUB] - beta)
                kt = kh[:lo_r] * jnp.exp2(beta - bq[:lo_r])
                sc_h.append(_dot_nt(qt.astype(BF16), kt.astype(BF16)).astype(BF16))
            scores.append(sc_h)
            vbs.append(vhb)
            b_last = bq[HG_CHUNK - 1:HG_CHUNK, :]
            kd = (kh * jnp.exp2(b_last - bq)).astype(BF16)
            states[h] = st * jnp.exp2(b_last) + _dot_tn(vhb, kd)
        for h in range(heads):
            offs = [jnp.zeros((HG_SUB, K), F32)]
            for blk in range(1, nsub):
                offs.append(_dot(scores[h][blk - 1], vbs[h][:blk * HG_SUB]))
            oc_ref[h, pl.ds(r0, HG_CHUNK), :] = o_inter[h] + jnp.concatenate(offs, axis=0)
    for h in range(heads):
        st_ref[h] = states[h]

    for h in range(heads):
        ls = slice(h * K, (h + 1) * K)
        g = g_ref[0, :, ls]
        o = od_ref[h] + oc_ref[h]
        y_ref[0, :, ls] = (_rms(o, gn_ref[:, ls]) * _silu(g)).astype(y_ref.dtype)


def _hgrn(proj3, lb_logits, gnorm, *, layer, width, t_pref=256):
    bsz, seq, _ = proj3.shape
    heads = width // HG_KDIM
    T = _tile(seq, t_pref)
    assert T % HG_CHUNK == 0 and T % (HG_SUB * HG_PAR) == 0 and width % (2 * HG_KDIM) == 0
    nl = lb_logits.shape[0]

    def col(k):
        return pl.BlockSpec((1, T, width), lambda b, t: (b, t, k))

    return pl.pallas_call(
        functools.partial(_hgrn_kernel, layer=layer, heads=heads),
        grid=(bsz, seq // T),
        in_specs=[col(0), col(1), col(2), col(3),
                  pl.BlockSpec((nl, width), lambda b, t: (0, 0)),
                  pl.BlockSpec((1, width), lambda b, t: (0, 0))],
        out_specs=pl.BlockSpec((1, T, width), lambda b, t: (b, t, 0)),
        out_shape=jax.ShapeDtypeStruct((bsz, seq, width), BF16),
        scratch_shapes=[
            pltpu.VMEM((heads, HG_KDIM, HG_KDIM), F32),
            pltpu.VMEM((heads, T, HG_KDIM), F32),
            pltpu.VMEM((heads, T, HG_KDIM), F32),
            pltpu.VMEM((heads, T, HG_KDIM), F32),
            pltpu.VMEM((heads, T, HG_KDIM), F32),
            pltpu.VMEM((heads, T, HG_KDIM), F32),
            pltpu.VMEM((heads, T, HG_KDIM), F32),
            pltpu.VMEM((HG_PAR, heads // 2, HG_SUB * HG_SUB, 2 * HG_KDIM), BF16),
        ],
        compiler_params=_params("parallel", "arbitrary"),
        name="hgrn2",
    )(proj3, proj3, proj3, proj3, lb_logits, gnorm)


def _shift_rows(x, d, fill):
    if d % SUBLANES == 0:
        return jnp.concatenate([jnp.full((d, x.shape[1]), fill, x.dtype), x[:x.shape[0] - d]], axis=0)
    rows = lax.broadcasted_iota(jnp.int32, (x.shape[0], 1), 0)
    return jnp.where(rows >= d, pltpu.roll(x, d, 0), fill)


def _rglru_kernel(x_ref, gate_ref, cw_ref, cb_ref, wa_ref, ba_ref, wx_ref, bx_ref, lam_ref, y_ref,
                  tail_ref, h_ref, *, blocks):
    T = x_ref.shape[1]
    W = x_ref.shape[2]
    bw = W // blocks
    t_idx = pl.program_id(1)

    @pl.when(t_idx == 0)
    def _():
        tail_ref[...] = jnp.zeros_like(tail_ref)
        h_ref[...] = jnp.zeros_like(h_ref)

    x = x_ref[0]
    ext = jnp.concatenate([tail_ref[...], x], axis=0)
    xc = cb_ref[...] + x * cw_ref[0:1, :]
    for j in range(1, CONV_WIDTH):
        xc = xc + pltpu.roll(ext, j, 0)[SUBLANES:, :] * cw_ref[j:j + 1, :]
    tail_ref[...] = x[T - SUBLANES:, :]

    xcb = xc.astype(BF16)
    ra, rx = [], []
    for n in range(blocks):
        blk = xcb[:, n * bw:(n + 1) * bw]
        ra.append(_dot(blk, wa_ref[n]))
        rx.append(_dot(blk, wx_ref[n]))
    rg = _logistic(jnp.concatenate(ra, axis=-1) + ba_ref[...])
    ig = _logistic(jnp.concatenate(rx, axis=-1) + bx_ref[...])

    lam = lam_ref[...]
    log_sig = -(jnp.maximum(-lam, 0.0) + jnp.log1p(jnp.exp(-jnp.abs(lam))))
    log_a = LRU_C * rg * log_sig
    a = jnp.exp(log_a)
    y = -jnp.tanh(log_a) * (a * a + 1.0)
    mult = jnp.where(y > 0.0, y * lax.rsqrt(y), 0.0)
    rows = lax.broadcasted_iota(jnp.int32, (T, 1), 0)
    mult = jnp.where((rows == 0) & (t_idx == 0), 1.0, mult)
    u = xc * ig * mult

    d = 1
    while d < T:
        u = a * _shift_rows(u, d, 0.0) + u
        a = a * _shift_rows(a, d, 1.0)
        d *= 2
    h = a * h_ref[...] + u
    h_ref[...] = h[T - 1:T, :]

    y_ref[0] = (h * jax.nn.gelu(gate_ref[0])).astype(y_ref.dtype)


def _rglru(proj3, conv_w, conv_b, wa, ba, wx, bx, lam, *, width, col0, t_pref=256):
    bsz, seq, _ = proj3.shape
    blocks = wa.shape[0]
    T = _tile(seq, t_pref)
    row = pl.BlockSpec((1, width), lambda b, t: (0, 0))
    wspec = pl.BlockSpec(wa.shape, lambda b, t: (0, 0, 0))
    return pl.pallas_call(
        functools.partial(_rglru_kernel, blocks=blocks),
        grid=(bsz, seq // T),
        in_specs=[
            pl.BlockSpec((1, T, width), lambda b, t: (b, t, col0)),
            pl.BlockSpec((1, T, width), lambda b, t: (b, t, col0 + 1)),
            pl.BlockSpec((CONV_WIDTH, width), lambda b, t: (0, 0)),
            row, wspec, row, wspec, row, row,
        ],
        out_specs=pl.BlockSpec((1, T, width), lambda b, t: (b, t, 0)),
        out_shape=jax.ShapeDtypeStruct((bsz, seq, width), BF16),
        scratch_shapes=[pltpu.VMEM((SUBLANES, width), F32), pltpu.VMEM((1, width), F32)],
        compiler_params=_params("parallel", "arbitrary"),
        name="rglru",
    )(proj3, proj3, conv_w, conv_b, wa, ba, wx, bx, lam)


def _merge_kernel(x_ref, ya_ref, yb_ref, ga_ref, gb_ref, bg_ref, wa_ref, wb_ref, wo_ref, o_ref):
    ga = jax.nn.sigmoid(ga_ref[...] + bg_ref[0:1, :])
    gb = jax.nn.sigmoid(gb_ref[...] + bg_ref[1:2, :])
    merged = ga * _dot(ya_ref[...], wa_ref[...]) + gb * _dot(yb_ref[...], wb_ref[...])
    o_ref[...] = x_ref[...] + _dot(merged.astype(BF16), wo_ref[...])


def _merge(x2, ya2, yb2, proj2, b_gate, w_a, w_b, w_o, *, gate_col0, tm_pref=256):
    m, d = x2.shape
    wa_w, wb_w = ya2.shape[1], yb2.shape[1]
    tm = _tile(m, tm_pref)
    return pl.pallas_call(
        _merge_kernel,
        grid=(m // tm,),
        in_specs=[
            pl.BlockSpec((tm, d), lambda i: (i, 0)),
            pl.BlockSpec((tm, wa_w), lambda i: (i, 0)),
            pl.BlockSpec((tm, wb_w), lambda i: (i, 0)),
            pl.BlockSpec((tm, d), lambda i: (i, gate_col0)),
            pl.BlockSpec((tm, d), lambda i: (i, gate_col0 + 1)),
            _resident(b_gate.shape), _resident(w_a.shape), _resident(w_b.shape), _resident(w_o.shape),
        ],
        out_specs=pl.BlockSpec((tm, d), lambda i: (i, 0)),
        out_shape=jax.ShapeDtypeStruct((m, d), F32),
        compiler_params=_params("parallel"),
        name="merge",
    )(x2, ya2, yb2, proj2, proj2, b_gate, w_a, w_b, w_o)


def _xattn_kernel(x_ref, g_ref, wq_ref, k_ref, v_ref, wo_ref, o_ref):
    x = x_ref[0]
    d = x.shape[-1]
    hd = d // XA_HEADS
    q = _dot(_rms(x, g_ref[...]).astype(BF16), wq_ref[...])
    heads = [slice(h * hd, (h + 1) * hd) for h in range(XA_HEADS)]
    scores = [_dot_nt(q[:, ls].astype(BF16), k_ref[0, :, ls]) * (hd ** -0.5) for ls in heads]
    probs = []
    for s in scores:
        p = jnp.exp(s - jnp.max(s, axis=-1, keepdims=True))
        probs.append((p / jnp.sum(p, axis=-1, keepdims=True)).astype(BF16))
    outs = [_dot(p, v_ref[0, :, ls]) for p, ls in zip(probs, heads)]
    o = jnp.concatenate(outs, axis=-1).astype(BF16)
    o_ref[0] = x + _dot(o, wo_ref[...])


def _xattn(x3, gain, wq, kv3, wo, *, tm_pref=512):
    bsz, seq, d = x3.shape
    nm = kv3.shape[1]
    tm = _tile(seq, tm_pref)
    return pl.pallas_call(
        _xattn_kernel,
        grid=(bsz, seq // tm),
        in_specs=[
            pl.BlockSpec((1, tm, d), lambda b, t: (b, t, 0)),
            _resident(gain.shape), _resident(wq.shape),
            pl.BlockSpec((1, nm, d), lambda b, t: (b, 0, 0)),
            pl.BlockSpec((1, nm, d), lambda b, t: (b, 0, 1)),
            _resident(wo.shape),
        ],
        out_specs=pl.BlockSpec((1, tm, d), lambda b, t: (b, t, 0)),
        out_shape=jax.ShapeDtypeStruct((bsz, seq, d), F32),
        compiler_params=_params("parallel", "parallel"),
        name="xattn",
    )(x3, gain, wq, kv3, kv3, wo)


def kernel(x, mem, ffn1_norm, ffn1_w_up, ffn1_w_down, mix_norm, w_in, b_gate, hgrn_lb_logits, hgrn_norm, conv_w, conv_b, lru_wa, lru_ba, lru_wx, lru_bx, lru_lambda, w_branch_a, w_branch_b, w_out, xattn_norm, mem_norm, xattn_wq, xattn_wkv, xattn_wo, ffn2_norm, ffn2_w_up, ffn2_w_down, final_norm):
    bsz, seq, d = x.shape
    nm = mem.shape[1]
    depth = ffn1_norm.shape[0]
    hg_w = hgrn_norm.shape[1]
    lru_w = lru_lambda.shape[1]
    assert hg_w == lru_w and 4 * hg_w + 2 * lru_w == 3 * d
    bf = lambda w: w.astype(BF16)
    row = lambda v: v.reshape(1, -1)
    fgain = row(final_norm)

    x2 = x.reshape(bsz * seq, d)
    mem2 = mem.reshape(bsz * nm, d)
    for l in range(depth):
        last = l == depth - 1
        x2 = _ffn(x2, row(ffn1_norm[l]), bf(ffn1_w_up[l]), bf(ffn1_w_down[l]), fgain, final_norm=False)

        proj2 = _norm_proj(x2, row(mix_norm[l]), bf(w_in[l]), F32, name="in_proj")
        proj3 = proj2.reshape(bsz, seq, -1)
        y_a = _hgrn(proj3, hgrn_lb_logits, row(hgrn_norm[l]), layer=l, width=hg_w)
        y_b = _rglru(proj3, conv_w[l], row(conv_b[l]), bf(lru_wa[l]), row(lru_ba[l]),
                     bf(lru_wx[l]), row(lru_bx[l]), row(lru_lambda[l]), width=lru_w, col0=4)
        x2 = _merge(x2, y_a.reshape(bsz * seq, hg_w), y_b.reshape(bsz * seq, lru_w), proj2,
                    b_gate[l], bf(w_branch_a[l]), bf(w_branch_b[l]), bf(w_out[l]), gate_col0=3)

        kv2 = _norm_proj(mem2, row(mem_norm[l]), bf(xattn_wkv[l]), BF16, name="kv_proj")
        x3 = _xattn(x2.reshape(bsz, seq, d), row(xattn_norm[l]), bf(xattn_wq[l]),
                    kv2.reshape(bsz, nm, 2 * d), bf(xattn_wo[l]))
        x2 = x3.reshape(bsz * seq, d)

        x2 = _ffn(x2, row(ffn2_norm[l]), bf(ffn2_w_up[l]), bf(ffn2_w_down[l]), fgain,
                  final_norm=last)
    return x2.reshape(bsz, seq, d)
```

```python
import functools

import jax
import jax.numpy as jnp
from jax import lax
from jax.experimental import pallas as pl
from jax.experimental.pallas import tpu as pltpu

EPS = 1e-6
HG_KDIM = 128
XA_HEADS = 4
LRU_C = 8.0
CONV_WIDTH = 4
LOG2E = 1.4426950408889634
LANES = 128
SUBLANES = 8
V7X_VMEM_BYTES = 64 * 1024 * 1024
VMEM_LIMIT = V7X_VMEM_BYTES - 8 * 1024 * 1024

HG_CHUNK = 64
HG_SUB = 16
HG_PAR = 4
FFN_SPLIT = 2

F32 = jnp.float32
BF16 = jnp.bfloat16


def _params(*semantics):
    return pltpu.CompilerParams(dimension_semantics=semantics, vmem_limit_bytes=VMEM_LIMIT)


def _tile(n, pref):
    t = min(n, pref)
    while n % t:
        t //= 2
    return t


def _rms(xf, g):
    ms = jnp.mean(xf * xf, axis=-1, keepdims=True)
    return xf * lax.rsqrt(ms + EPS) * g


def _dot(a, b):
    return jnp.dot(a, b, preferred_element_type=F32)


def _dot_nt(a, b):
    return lax.dot_general(a, b, (((1,), (1,)), ((), ())), preferred_element_type=F32)


def _dot_tn(a, b):
    return lax.dot_general(a, b, (((0,), (0,)), ((), ())), preferred_element_type=F32)


def _resident(shape):
    return pl.BlockSpec(shape, lambda *_: (0,) * len(shape), pipeline_mode=pl.Buffered(1))


def _ffn_kernel(x_ref, g_ref, wg_ref, wu_ref, wd_ref, fg_ref, o_ref, hn_ref, *, final_norm):
    j = pl.program_id(1)

    @pl.when(j == 0)
    def _():
        x = x_ref[...]
        hn_ref[...] = _rms(x, g_ref[...]).astype(BF16)
        o_ref[...] = x

    h = hn_ref[...]
    half = wg_ref.shape[1] // FFN_SPLIT
    acts = []
    for c in range(FFN_SPLIT):
        cs = slice(c * half, (c + 1) * half)
        gate = _dot(h, wg_ref[:, cs])
        up = _dot(h, wu_ref[:, cs])
        acts.append((0.5 * (gate * jax.nn.sigmoid(gate) * up)).astype(BF16))
    for c in range(FFN_SPLIT):
        o_ref[...] += _dot(acts[c], wd_ref[c * half:(c + 1) * half, :])

    if final_norm:
        @pl.when(j == pl.num_programs(1) - 1)
        def _():
            o_ref[...] = _rms(o_ref[...], fg_ref[...])


def _ffn(x2, gain, w_up, w_down, final_gain, *, final_norm, tm_pref=1024, tf_pref=512):
    m, d = x2.shape
    dff = w_down.shape[0]
    tm = _tile(m, tm_pref)
    tf = _tile(dff, tf_pref)
    nf = dff // tf
    return pl.pallas_call(
        functools.partial(_ffn_kernel, final_norm=final_norm),
        grid=(m // tm, nf),
        in_specs=[
            pl.BlockSpec((tm, d), lambda i, j: (i, 0)),
            pl.BlockSpec((1, d), lambda i, j: (0, 0)),
            pl.BlockSpec((d, tf), lambda i, j: (0, j)),
            pl.BlockSpec((d, tf), lambda i, j: (0, j + nf)),
            pl.BlockSpec((tf, d), lambda i, j: (j, 0)),
            pl.BlockSpec((1, d), lambda i, j: (0, 0)),
        ],
        out_specs=pl.BlockSpec((tm, d), lambda i, j: (i, 0)),
        out_shape=jax.ShapeDtypeStruct((m, d), F32),
        scratch_shapes=[pltpu.VMEM((tm, d), BF16)],
        compiler_params=_params("parallel", "arbitrary"),
        name="ffn_final" if final_norm else "ffn",
    )(x2, gain, w_up, w_up, w_down, final_gain)


def _norm_proj_kernel(x_ref, g_ref, w_ref, o_ref, hn_ref):
    @pl.when(pl.program_id(1) == 0)
    def _():
        hn_ref[...] = _rms(x_ref[...], g_ref[...]).astype(BF16)

    o_ref[...] = _dot(hn_ref[...], w_ref[...]).astype(o_ref.dtype)


def _norm_proj(x2, gain, w, out_dtype, *, name, tm_pref=1024, tn_pref=1024):
    m, d = x2.shape
    n = w.shape[1]
    tm = _tile(m, tm_pref)
    tn = _tile(n, tn_pref)
    return pl.pallas_call(
        _norm_proj_kernel,
        grid=(m // tm, n // tn),
        in_specs=[
            pl.BlockSpec((tm, d), lambda i, j: (i, 0)),
            pl.BlockSpec((1, d), lambda i, j: (0, 0)),
            pl.BlockSpec((d, tn), lambda i, j: (0, j)),
        ],
        out_specs=pl.BlockSpec((tm, tn), lambda i, j: (i, j)),
        out_shape=jax.ShapeDtypeStruct((m, n), out_dtype),
        scratch_shapes=[pltpu.VMEM((tm, d), BF16)],
        compiler_params=_params("parallel", "arbitrary"),
        name=name,
    )(x2, gain, w)


def _logistic(x):
    return 0.5 * jnp.tanh(0.5 * x) + 0.5


def _silu(x):
    h = 0.5 * x
    return h + h * jnp.tanh(h)


def _hgrn_kernel(q_ref, f_ref, v_ref, g_ref, lbl_ref, gn_ref, y_ref,
                 st_ref, qf_ref, kk_ref, vv_ref, b_ref, od_ref, oc_ref, es_ref, *, layer, heads):
    T = q_ref.shape[1]
    K = HG_KDIM
    H8 = SUBLANES

    @pl.when(pl.program_id(1) == 0)
    def _():
        st_ref[...] = jnp.zeros_like(st_ref)

    lg = lbl_ref[...]
    ex = jnp.exp(lg - jnp.max(lg, axis=0, keepdims=True))
    sm = ex / jnp.sum(ex, axis=0, keepdims=True)
    lb = jnp.sum(sm[:layer + 1], axis=0, keepdims=True)

    r = lax.broadcasted_iota(jnp.int32, (T, T), 0)
    c = lax.broadcasted_iota(jnp.int32, (T, T), 1)
    tri = jnp.where((c <= r) & (r // HG_CHUNK == c // HG_CHUNK), 1.0, 0.0).astype(BF16)
    f_mid = 0.5 * (1.0 + lb)
    f_half = 0.5 * (1.0 - lb)
    for h in range(heads):
        ls = slice(h * K, (h + 1) * K)
        qf_ref[h] = _silu(q_ref[0, :, ls])
        swing = f_half[:, ls] * jnp.tanh(0.5 * f_ref[0, :, ls])
        f = f_mid[:, ls] + swing
        kk_ref[h] = f_half[:, ls] - swing
        vv_ref[h] = v_ref[0, :, ls]
        logf = jnp.log(f) * LOG2E
        hi = logf.astype(BF16)
        lo = (logf - hi.astype(F32)).astype(BF16)
        b_ref[h] = _dot(tri, hi) + _dot(tri, lo)

    pr = lax.broadcasted_iota(jnp.int32, (2 * K, 2 * K), 0)
    pc = lax.broadcasted_iota(jnp.int32, (2 * K, 2 * K), 1)
    seg_ones = jnp.where(pr // K == pc // K, 1.0, 0.0).astype(BF16)
    trow = lax.broadcasted_iota(jnp.int32, (H8, 1), 0)
    zeros8 = jnp.zeros((H8, K), F32)

    def diag_block(r0, e_ref):
        for p in range(heads // 2):
            for hh in range(2):
                h = 2 * p + hh
                q_lo = qf_ref[h, pl.ds(r0, H8), :]
                q_hi = qf_ref[h, pl.ds(r0 + H8, H8), :]
                b_lo = b_ref[h, pl.ds(r0, H8), :]
                b_hi = b_ref[h, pl.ds(r0 + H8, H8), :]
                for s in range(HG_SUB):
                    ks = kk_ref[h, pl.ds(r0 + s, 1), :]
                    bs = b_ref[h, pl.ds(r0 + s, 1), :]
                    e_hi = jnp.exp2(b_hi - bs) * (q_hi * ks)
                    if s == 0:
                        e_lo = jnp.exp2(b_lo - bs) * (q_lo * ks)
                    elif s < H8:
                        e_lo = jnp.where(trow >= s, jnp.exp2(b_lo - bs) * (q_lo * ks), 0.0)
                    else:
                        e_lo = zeros8
                        if s > H8:
                            e_hi = jnp.where(trow >= s - H8, e_hi, 0.0)
                    e_ref[p, s * HG_SUB:(s + 1) * HG_SUB, hh * K:(hh + 1) * K] = (
                        jnp.concatenate([e_lo, e_hi], axis=0).astype(BF16))
            w = _dot(e_ref[p], seg_ones)
            for hh in range(2):
                h = 2 * p + hh
                o_lo = jnp.zeros((H8, K), F32)
                o_hi = jnp.zeros((H8, K), F32)
                for s in range(HG_SUB):
                    vs = vv_ref[h, pl.ds(r0 + s, 1), :]
                    if s < H8:
                        o_lo = o_lo + w[s * HG_SUB:s * HG_SUB + H8, hh * K:(hh + 1) * K] * vs
                    o_hi = o_hi + w[s * HG_SUB + H8:(s + 1) * HG_SUB, hh * K:(hh + 1) * K] * vs
                od_ref[h, pl.ds(r0, H8), :] = o_lo
                od_ref[h, pl.ds(r0 + H8, H8), :] = o_hi

    n_par = es_ref.shape[0]

    def diag_group(gi, carry):
        for u in range(n_par):
            diag_block(pl.multiple_of((gi * n_par + u) * HG_SUB, HG_SUB), es_ref.at[u])
        return carry

    lax.fori_loop(0, T // (HG_SUB * n_par), diag_group, 0)

    nsub = HG_CHUNK // HG_SUB

    states = [st_ref[h] for h in range(heads)]
    for ci in range(T // HG_CHUNK):
        r0 = ci * HG_CHUNK
        o_inter, scores, vbs = [], [], []
        for h in range(heads):
            bq = b_ref[h, pl.ds(r0, HG_CHUNK), :]
            qh = qf_ref[h, pl.ds(r0, HG_CHUNK), :]
            kh = kk_ref[h, pl.ds(r0, HG_CHUNK), :]
            vhb = vv_ref[h, pl.ds(r0, HG_CHUNK), :].astype(BF16)
            st = states[h]
            o_inter.append(_dot_nt((qh * jnp.exp2(bq)).astype(BF16), st.astype(BF16)))
            sc_h = []
            for blk in range(1, nsub):
                lo_r = blk * HG_SUB
                beta = bq[lo_r - 1:lo_r, :]
                qt = qh[lo_r:lo_r + HG_SUB] * jnp.exp2(bq[lo_r:lo_r + HG_SUB] - beta)
                kt = kh[:lo_r] * jnp.exp2(beta - bq[:lo_r])
                sc_h.append(_dot_nt(qt.astype(BF16), kt.astype(BF16)).astype(BF16))
            scores.append(sc_h)
            vbs.append(vhb)
            b_last = bq[HG_CHUNK - 1:HG_CHUNK, :]
            kd = (kh * jnp.exp2(b_last - bq)).astype(BF16)
            states[h] = st * jnp.exp2(b_last) + _dot_tn(vhb, kd)
        for h in range(heads):
            offs = [jnp.zeros((HG_SUB, K), F32)]
            for blk in range(1, nsub):
                offs.append(_dot(scores[h][blk - 1], vbs[h][:blk * HG_SUB]))
            oc_ref[h, pl.ds(r0, HG_CHUNK), :] = o_inter[h] + jnp.concatenate(offs, axis=0)
    for h in range(heads):
        st_ref[h] = states[h]

    for h in range(heads):
        ls = slice(h * K, (h + 1) * K)
        g = g_ref[0, :, ls]
        o = od_ref[h] + oc_ref[h]
        y_ref[0, :, ls] = (_rms(o, gn_ref[:, ls]) * _silu(g)).astype(y_ref.dtype)


def _hgrn(proj3, lb_logits, gnorm, *, layer, width, t_pref=256):
    bsz, seq, _ = proj3.shape
    heads = width // HG_KDIM
    T = _tile(seq, t_pref)
    assert T % HG_CHUNK == 0 and T % (HG_SUB * HG_PAR) == 0 and width % (2 * HG_KDIM) == 0
    nl = lb_logits.shape[0]

    def col(k):
        return pl.BlockSpec((1, T, width), lambda b, t: (b, t, k))

    return pl.pallas_call(
        functools.partial(_hgrn_kernel, layer=layer, heads=heads),
        grid=(bsz, seq // T),
        in_specs=[col(0), col(1), col(2), col(3),
                  pl.BlockSpec((nl, width), lambda b, t: (0, 0)),
                  pl.BlockSpec((1, width), lambda b, t: (0, 0))],
        out_specs=pl.BlockSpec((1, T, width), lambda b, t: (b, t, 0)),
        out_shape=jax.ShapeDtypeStruct((bsz, seq, width), BF16),
        scratch_shapes=[
            pltpu.VMEM((heads, HG_KDIM, HG_KDIM), F32),
            pltpu.VMEM((heads, T, HG_KDIM), F32),
            pltpu.VMEM((heads, T, HG_KDIM), F32),
            pltpu.VMEM((heads, T, HG_KDIM), F32),
            pltpu.VMEM((heads, T, HG_KDIM), F32),
            pltpu.VMEM((heads, T, HG_KDIM), F32),
            pltpu.VMEM((heads, T, HG_KDIM), F32),
            pltpu.VMEM((HG_PAR, heads // 2, HG_SUB * HG_SUB, 2 * HG_KDIM), BF16),
        ],
        compiler_params=_params("parallel", "arbitrary"),
        name="hgrn2",
    )(proj3, proj3, proj3, proj3, lb_logits, gnorm)


def _rglru_kernel(x_ref, gate_ref, cw_ref, cb_ref, wa_ref, ba_ref, wx_ref, bx_ref, lam_ref, y_ref,
                  tail_ref, h_ref, *, blocks):
    T = x_ref.shape[1]
    W = x_ref.shape[2]
    bw = W // blocks
    t_idx = pl.program_id(1)

    @pl.when(t_idx == 0)
    def _():
        tail_ref[...] = jnp.zeros_like(tail_ref)
        h_ref[...] = jnp.zeros_like(h_ref)

    x = x_ref[0]
    ext = jnp.concatenate([tail_ref[...], x], axis=0)
    xc = cb_ref[...] + x * cw_ref[0:1, :]
    for j in range(1, CONV_WIDTH):
        xc = xc + pltpu.roll(ext, j, 0)[SUBLANES:, :] * cw_ref[j:j + 1, :]
    tail_ref[...] = x[T - SUBLANES:, :]

    xcb = xc.astype(BF16)
    ra, rx = [], []
    for n in range(blocks):
        blk = xcb[:, n * bw:(n + 1) * bw]
        ra.append(_dot(blk, wa_ref[n]))
        rx.append(_dot(blk, wx_ref[n]))
    rg = _logistic(jnp.concatenate(ra, axis=-1) + ba_ref[...])
    ig = _logistic(jnp.concatenate(rx, axis=-1) + bx_ref[...])

    lam = lam_ref[...]
    log_sig = -(jnp.maximum(-lam, 0.0) + jnp.log1p(jnp.exp(-jnp.abs(lam))))
    log_a = LRU_C * rg * log_sig
    a = jnp.exp(log_a)
    y = -jnp.tanh(log_a) * (a * a + 1.0)
    mult = jnp.where(y > 0.0, y * lax.rsqrt(y), 0.0)
    rows = lax.broadcasted_iota(jnp.int32, (T, 1), 0)
    mult = jnp.where((rows == 0) & (t_idx == 0), 1.0, mult)
    u = xc * ig * mult

    groups = T // SUBLANES
    a3 = a.reshape(groups, SUBLANES, W)
    u3 = u.reshape(groups, SUBLANES, W)
    sub = lax.broadcasted_iota(jnp.int32, (1, SUBLANES, 1), 1)
    d = 1
    while d < SUBLANES:
        keep = sub >= d
        u3 = a3 * jnp.where(keep, pltpu.roll(u3, d, 1), 0.0) + u3
        a3 = a3 * jnp.where(keep, pltpu.roll(a3, d, 1), 1.0)
        d *= 2
    carry = h_ref[...]
    hs = []
    for g in range(groups):
        hg = a3[g] * carry + u3[g]
        hs.append(hg)
        carry = hg[SUBLANES - 1:SUBLANES, :]
    h = jnp.concatenate(hs, axis=0)
    h_ref[...] = carry

    y_ref[0] = (h * jax.nn.gelu(gate_ref[0])).astype(y_ref.dtype)


def _rglru(proj3, conv_w, conv_b, wa, ba, wx, bx, lam, *, width, col0, t_pref=256):
    bsz, seq, _ = proj3.shape
    blocks = wa.shape[0]
    T = _tile(seq, t_pref)
    row = pl.BlockSpec((1, width), lambda b, t: (0, 0))
    wspec = pl.BlockSpec(wa.shape, lambda b, t: (0, 0, 0))
    return pl.pallas_call(
        functools.partial(_rglru_kernel, blocks=blocks),
        grid=(bsz, seq // T),
        in_specs=[
            pl.BlockSpec((1, T, width), lambda b, t: (b, t, col0)),
            pl.BlockSpec((1, T, width), lambda b, t: (b, t, col0 + 1)),
            pl.BlockSpec((CONV_WIDTH, width), lambda b, t: (0, 0)),
            row, wspec, row, wspec, row, row,
        ],
        out_specs=pl.BlockSpec((1, T, width), lambda b, t: (b, t, 0)),
        out_shape=jax.ShapeDtypeStruct((bsz, seq, width), BF16),
        scratch_shapes=[pltpu.VMEM((SUBLANES, width), F32), pltpu.VMEM((1, width), F32)],
        compiler_params=_params("parallel", "arbitrary"),
        name="rglru",
    )(proj3, proj3, conv_w, conv_b, wa, ba, wx, bx, lam)


def _merge_kernel(x_ref, ya_ref, yb_ref, ga_ref, gb_ref, bg_ref, wa_ref, wb_ref, wo_ref, o_ref):
    ga = jax.nn.sigmoid(ga_ref[...] + bg_ref[0:1, :])
    gb = jax.nn.sigmoid(gb_ref[...] + bg_ref[1:2, :])
    merged = ga * _dot(ya_ref[...], wa_ref[...]) + gb * _dot(yb_ref[...], wb_ref[...])
    o_ref[...] = x_ref[...] + _dot(merged.astype(BF16), wo_ref[...])


def _merge(x2, ya2, yb2, proj2, b_gate, w_a, w_b, w_o, *, gate_col0, tm_pref=256):
    m, d = x2.shape
    wa_w, wb_w = ya2.shape[1], yb2.shape[1]
    tm = _tile(m, tm_pref)
    return pl.pallas_call(
        _merge_kernel,
        grid=(m // tm,),
        in_specs=[
            pl.BlockSpec((tm, d), lambda i: (i, 0)),
            pl.BlockSpec((tm, wa_w), lambda i: (i, 0)),
            pl.BlockSpec((tm, wb_w), lambda i: (i, 0)),
            pl.BlockSpec((tm, d), lambda i: (i, gate_col0)),
            pl.BlockSpec((tm, d), lambda i: (i, gate_col0 + 1)),
            _resident(b_gate.shape), _resident(w_a.shape), _resident(w_b.shape), _resident(w_o.shape),
        ],
        out_specs=pl.BlockSpec((tm, d), lambda i: (i, 0)),
        out_shape=jax.ShapeDtypeStruct((m, d), F32),
        compiler_params=_params("parallel"),
        name="merge",
    )(x2, ya2, yb2, proj2, proj2, b_gate, w_a, w_b, w_o)


def _xattn_kernel(x_ref, g_ref, wq_ref, k_ref, v_ref, wo_ref, o_ref):
    x = x_ref[0]
    d = x.shape[-1]
    hd = d // XA_HEADS
    q = _dot(_rms(x, g_ref[...]).astype(BF16), wq_ref[...])
    heads = [slice(h * hd, (h + 1) * hd) for h in range(XA_HEADS)]
    scores = [_dot_nt(q[:, ls].astype(BF16), k_ref[0, :, ls]) * (hd ** -0.5) for ls in heads]
    probs = []
    for s in scores:
        p = jnp.exp(s - jnp.max(s, axis=-1, keepdims=True))
        probs.append((p / jnp.sum(p, axis=-1, keepdims=True)).astype(BF16))
    outs = [_dot(p, v_ref[0, :, ls]) for p, ls in zip(probs, heads)]
    o = jnp.concatenate(outs, axis=-1).astype(BF16)
    o_ref[0] = x + _dot(o, wo_ref[...])


def _xattn(x3, gain, wq, kv3, wo, *, tm_pref=512):
    bsz, seq, d = x3.shape
    nm = kv3.shape[1]
    tm = _tile(seq, tm_pref)
    return pl.pallas_call(
        _xattn_kernel,
        grid=(bsz, seq // tm),
        in_specs=[
            pl.BlockSpec((1, tm, d), lambda b, t: (b, t, 0)),
            _resident(gain.shape), _resident(wq.shape),
            pl.BlockSpec((1, nm, d), lambda b, t: (b, 0, 0)),
            pl.BlockSpec((1, nm, d), lambda b, t: (b, 0, 1)),
            _resident(wo.shape),
        ],
        out_specs=pl.BlockSpec((1, tm, d), lambda b, t: (b, t, 0)),
        out_shape=jax.ShapeDtypeStruct((bsz, seq, d), F32),
        compiler_params=_params("parallel", "parallel"),
        name="xattn",
    )(x3, gain, wq, kv3, kv3, wo)


def kernel(x, mem, ffn1_norm, ffn1_w_up, ffn1_w_down, mix_norm, w_in, b_gate, hgrn_lb_logits, hgrn_norm, conv_w, conv_b, lru_wa, lru_ba, lru_wx, lru_bx, lru_lambda, w_branch_a, w_branch_b, w_out, xattn_norm, mem_norm, xattn_wq, xattn_wkv, xattn_wo, ffn2_norm, ffn2_w_up, ffn2_w_down, final_norm):
    bsz, seq, d = x.shape
    nm = mem.shape[1]
    depth = ffn1_norm.shape[0]
    hg_w = hgrn_norm.shape[1]
    lru_w = lru_lambda.shape[1]
    assert hg_w == lru_w and 4 * hg_w + 2 * lru_w == 3 * d
    bf = lambda w: w.astype(BF16)
    row = lambda v: v.reshape(1, -1)
    fgain = row(final_norm)

    x2 = x.reshape(bsz * seq, d)
    mem2 = mem.reshape(bsz * nm, d)
    for l in range(depth):
        last = l == depth - 1
        x2 = _ffn(x2, row(ffn1_norm[l]), bf(ffn1_w_up[l]), bf(ffn1_w_down[l]), fgain, final_norm=False)

        proj2 = _norm_proj(x2, row(mix_norm[l]), bf(w_in[l]), F32, name="in_proj",
                           tm_pref=512, tn_pref=2560)
        proj3 = proj2.reshape(bsz, seq, -1)
        y_a = _hgrn(proj3, hgrn_lb_logits, row(hgrn_norm[l]), layer=l, width=hg_w)
        y_b = _rglru(proj3, conv_w[l], row(conv_b[l]), bf(lru_wa[l]), row(lru_ba[l]),
                     bf(lru_wx[l]), row(lru_bx[l]), row(lru_lambda[l]), width=lru_w, col0=4)
        x2 = _merge(x2, y_a.reshape(bsz * seq, hg_w), y_b.reshape(bsz * seq, lru_w), proj2,
                    b_gate[l], bf(w_branch_a[l]), bf(w_branch_b[l]), bf(w_out[l]), gate_col0=3)

        kv2 = _norm_proj(mem2, row(mem_norm[l]), bf(xattn_wkv[l]), BF16, name="kv_proj")
        x3 = _xattn(x2.reshape(bsz, seq, d), row(xattn_norm[l]), bf(xattn_wq[l]),
                    kv2.reshape(bsz, nm, 2 * d), bf(xattn_wo[l]))
        x2 = x3.reshape(bsz * seq, d)

        x2 = _ffn(x2, row(ffn2_norm[l]), bf(ffn2_w_up[l]), bf(ffn2_w_down[l]), fgain,
                  final_norm=last)
    return x2.reshape(bsz, seq, d)
```

```python
import functools

import jax
import jax.numpy as jnp
from jax import lax
from jax.experimental import pallas as pl
from jax.experimental.pallas import tpu as pltpu

EPS = 1e-6
HG_KDIM = 128
XA_HEADS = 4
LRU_C = 8.0
CONV_WIDTH = 4
LOG2E = 1.4426950408889634
LANES = 128
SUBLANES = 8
V7X_VMEM_BYTES = 64 * 1024 * 1024
VMEM_LIMIT = V7X_VMEM_BYTES - 4 * 1024 * 1024

HG_CHUNK = 64
HG_SUB = 16
HG_PAR = 4
FFN_SPLIT = 2

F32 = jnp.float32
BF16 = jnp.bfloat16


def _params(*semantics):
    return pltpu.CompilerParams(dimension_semantics=semantics, vmem_limit_bytes=VMEM_LIMIT)


def _tile(n, pref):
    t = min(n, pref)
    while n % t:
        t //= 2
    return t


def _rms(xf, g):
    ms = jnp.mean(xf * xf, axis=-1, keepdims=True)
    return xf * lax.rsqrt(ms + EPS) * g


def _dot(a, b):
    return jnp.dot(a, b, preferred_element_type=F32)


def _dot_nt(a, b):
    return lax.dot_general(a, b, (((1,), (1,)), ((), ())), preferred_element_type=F32)


def _dot_tn(a, b):
    return lax.dot_general(a, b, (((0,), (0,)), ((), ())), preferred_element_type=F32)


def _resident(shape):
    return pl.BlockSpec(shape, lambda *_: (0,) * len(shape), pipeline_mode=pl.Buffered(1))


def _ffn_kernel(x_ref, g_ref, wg_ref, wu_ref, wd_ref, fg_ref, o_ref, hn_ref, *, final_norm):
    j = pl.program_id(1)

    @pl.when(j == 0)
    def _():
        x = x_ref[...]
        hn_ref[...] = _rms(x, g_ref[...]).astype(BF16)
        o_ref[...] = x

    h = hn_ref[...]
    half = wg_ref.shape[1] // FFN_SPLIT
    acts = []
    for c in range(FFN_SPLIT):
        cs = slice(c * half, (c + 1) * half)
        gate = _dot(h, wg_ref[:, cs])
        up = _dot(h, wu_ref[:, cs])
        acts.append((0.5 * (gate * jax.nn.sigmoid(gate) * up)).astype(BF16))
    for c in range(FFN_SPLIT):
        o_ref[...] += _dot(acts[c], wd_ref[c * half:(c + 1) * half, :])

    if final_norm:
        @pl.when(j == pl.num_programs(1) - 1)
        def _():
            o_ref[...] = _rms(o_ref[...], fg_ref[...])


def _ffn(x2, gain, w_up, w_down, final_gain, *, final_norm, tm_pref=1024, tf_pref=512):
    m, d = x2.shape
    dff = w_down.shape[0]
    tm = _tile(m, tm_pref)
    tf = _tile(dff, tf_pref)
    nf = dff // tf
    return pl.pallas_call(
        functools.partial(_ffn_kernel, final_norm=final_norm),
        grid=(m // tm, nf),
        in_specs=[
            pl.BlockSpec((tm, d), lambda i, j: (i, 0)),
            pl.BlockSpec((1, d), lambda i, j: (0, 0)),
            pl.BlockSpec((d, tf), lambda i, j: (0, j)),
            pl.BlockSpec((d, tf), lambda i, j: (0, j + nf)),
            pl.BlockSpec((tf, d), lambda i, j: (j, 0)),
            pl.BlockSpec((1, d), lambda i, j: (0, 0)),
        ],
        out_specs=pl.BlockSpec((tm, d), lambda i, j: (i, 0)),
        out_shape=jax.ShapeDtypeStruct((m, d), F32),
        scratch_shapes=[pltpu.VMEM((tm, d), BF16)],
        compiler_params=_params("parallel", "arbitrary"),
        name="ffn_final" if final_norm else "ffn",
    )(x2, gain, w_up, w_up, w_down, final_gain)


def _norm_proj_kernel(x_ref, g_ref, w_ref, o_ref, hn_ref):
    @pl.when(pl.program_id(1) == 0)
    def _():
        hn_ref[...] = _rms(x_ref[...], g_ref[...]).astype(BF16)

    o_ref[...] = _dot(hn_ref[...], w_ref[...]).astype(o_ref.dtype)


def _norm_proj(x2, gain, w, out_dtype, *, name, tm_pref=1024, tn_pref=1024):
    m, d = x2.shape
    n = w.shape[1]
    tm = _tile(m, tm_pref)
    tn = _tile(n, tn_pref)
    return pl.pallas_call(
        _norm_proj_kernel,
        grid=(m // tm, n // tn),
        in_specs=[
            pl.BlockSpec((tm, d), lambda i, j: (i, 0)),
            pl.BlockSpec((1, d), lambda i, j: (0, 0)),
            pl.BlockSpec((d, tn), lambda i, j: (0, j)),
        ],
        out_specs=pl.BlockSpec((tm, tn), lambda i, j: (i, j)),
        out_shape=jax.ShapeDtypeStruct((m, n), out_dtype),
        scratch_shapes=[pltpu.VMEM((tm, d), BF16)],
        compiler_params=_params("parallel", "arbitrary"),
        name=name,
    )(x2, gain, w)


def _logistic(x):
    return 0.5 * jnp.tanh(0.5 * x) + 0.5


def _silu(x):
    h = 0.5 * x
    return h + h * jnp.tanh(h)


def _hgrn_kernel(q_ref, f_ref, v_ref, g_ref, lbl_ref, gn_ref, y_ref,
                 st_ref, qf_ref, kk_ref, vv_ref, b_ref, od_ref, oc_ref, es_ref, *, layer, heads):
    T = q_ref.shape[1]
    K = HG_KDIM
    H8 = SUBLANES

    @pl.when(pl.program_id(1) == 0)
    def _():
        st_ref[...] = jnp.zeros_like(st_ref)

    lg = lbl_ref[...]
    ex = jnp.exp(lg - jnp.max(lg, axis=0, keepdims=True))
    sm = ex / jnp.sum(ex, axis=0, keepdims=True)
    lb = jnp.sum(sm[:layer + 1], axis=0, keepdims=True)

    r = lax.broadcasted_iota(jnp.int32, (T, T), 0)
    c = lax.broadcasted_iota(jnp.int32, (T, T), 1)
    tri = jnp.where((c <= r) & (r // HG_CHUNK == c // HG_CHUNK), 1.0, 0.0).astype(BF16)
    f_mid = 0.5 * (1.0 + lb)
    f_half = 0.5 * (1.0 - lb)
    for h in range(heads):
        ls = slice(h * K, (h + 1) * K)
        qf_ref[h] = _silu(q_ref[0, :, ls])
        swing = f_half[:, ls] * jnp.tanh(0.5 * f_ref[0, :, ls])
        f = f_mid[:, ls] + swing
        kk_ref[h] = f_half[:, ls] - swing
        vv_ref[h] = v_ref[0, :, ls]
        logf = jnp.log(f) * LOG2E
        hi = logf.astype(BF16)
        lo = (logf - hi.astype(F32)).astype(BF16)
        b_ref[h] = _dot(tri, hi) + _dot(tri, lo)

    pr = lax.broadcasted_iota(jnp.int32, (2 * K, 2 * K), 0)
    pc = lax.broadcasted_iota(jnp.int32, (2 * K, 2 * K), 1)
    seg_ones = jnp.where(pr // K == pc // K, 1.0, 0.0).astype(BF16)
    trow = lax.broadcasted_iota(jnp.int32, (H8, 1), 0)
    zeros8 = jnp.zeros((H8, K), F32)

    def diag_block(r0, e_ref):
        for p in range(heads // 2):
            for hh in range(2):
                h = 2 * p + hh
                q_lo = qf_ref[h, pl.ds(r0, H8), :]
                q_hi = qf_ref[h, pl.ds(r0 + H8, H8), :]
                b_lo = b_ref[h, pl.ds(r0, H8), :]
                b_hi = b_ref[h, pl.ds(r0 + H8, H8), :]
                for s in range(HG_SUB):
                    ks = kk_ref[h, pl.ds(r0 + s, 1), :]
                    bs = b_ref[h, pl.ds(r0 + s, 1), :]
                    e_hi = jnp.exp2(b_hi - bs) * (q_hi * ks)
                    if s == 0:
                        e_lo = jnp.exp2(b_lo - bs) * (q_lo * ks)
                    elif s < H8:
                        e_lo = jnp.where(trow >= s, jnp.exp2(b_lo - bs) * (q_lo * ks), 0.0)
                    else:
                        e_lo = zeros8
                        if s > H8:
                            e_hi = jnp.where(trow >= s - H8, e_hi, 0.0)
                    e_ref[p, s * HG_SUB:(s + 1) * HG_SUB, hh * K:(hh + 1) * K] = (
                        jnp.concatenate([e_lo, e_hi], axis=0).astype(BF16))
            w = _dot(e_ref[p], seg_ones)
            for hh in range(2):
                h = 2 * p + hh
                o_lo = jnp.zeros((H8, K), F32)
                o_hi = jnp.zeros((H8, K), F32)
                for s in range(HG_SUB):
                    vs = vv_ref[h, pl.ds(r0 + s, 1), :]
                    if s < H8:
                        o_lo = o_lo + w[s * HG_SUB:s * HG_SUB + H8, hh * K:(hh + 1) * K] * vs
                    o_hi = o_hi + w[s * HG_SUB + H8:(s + 1) * HG_SUB, hh * K:(hh + 1) * K] * vs
                od_ref[h, pl.ds(r0, H8), :] = o_lo
                od_ref[h, pl.ds(r0 + H8, H8), :] = o_hi

    n_par = es_ref.shape[0]

    def diag_group(gi, carry):
        for u in range(n_par):
            diag_block(pl.multiple_of((gi * n_par + u) * HG_SUB, HG_SUB), es_ref.at[u])
        return carry

    lax.fori_loop(0, T // (HG_SUB * n_par), diag_group, 0)

    nsub = HG_CHUNK // HG_SUB

    states = [st_ref[h] for h in range(heads)]
    for ci in range(T // HG_CHUNK):
        r0 = ci * HG_CHUNK
        o_inter, scores, vbs = [], [], []
        for h in range(heads):
            bq = b_ref[h, pl.ds(r0, HG_CHUNK), :]
            qh = qf_ref[h, pl.ds(r0, HG_CHUNK), :]
            kh = kk_ref[h, pl.ds(r0, HG_CHUNK), :]
            vhb = vv_ref[h, pl.ds(r0, HG_CHUNK), :].astype(BF16)
            st = states[h]
            o_inter.append(_dot_nt((qh * jnp.exp2(bq)).astype(BF16), st.astype(BF16)))
            sc_h = []
            for blk in range(1, nsub):
                lo_r = blk * HG_SUB
                beta = bq[lo_r - 1:lo_r, :]
                qt = qh[lo_r:lo_r + HG_SUB] * jnp.exp2(bq[lo_r:lo_r + HG_SUB] - beta)
                kt = kh[:lo_r] * jnp.exp2(beta - bq[:lo_r])
                sc_h.append(_dot_nt(qt.astype(BF16), kt.astype(BF16)).astype(BF16))
            scores.append(sc_h)
            vbs.append(vhb)
            b_last = bq[HG_CHUNK - 1:HG_CHUNK, :]
            kd = (kh * jnp.exp2(b_last - bq)).astype(BF16)
            states[h] = st * jnp.exp2(b_last) + _dot_tn(vhb, kd)
        for h in range(heads):
            offs = [jnp.zeros((HG_SUB, K), F32)]
            for blk in range(1, nsub):
                offs.append(_dot(scores[h][blk - 1], vbs[h][:blk * HG_SUB]))
            oc_ref[h, pl.ds(r0, HG_CHUNK), :] = o_inter[h] + jnp.concatenate(offs, axis=0)
    for h in range(heads):
        st_ref[h] = states[h]

    for h in range(heads):
        ls = slice(h * K, (h + 1) * K)
        g = g_ref[0, :, ls]
        o = od_ref[h] + oc_ref[h]
        y_ref[0, :, ls] = (_rms(o, gn_ref[:, ls]) * _silu(g)).astype(y_ref.dtype)


def _hgrn(proj3, lb_logits, gnorm, *, layer, width, t_pref=256):
    bsz, seq, _ = proj3.shape
    heads = width // HG_KDIM
    T = _tile(seq, t_pref)
    assert T % HG_CHUNK == 0 and T % (HG_SUB * HG_PAR) == 0 and width % (2 * HG_KDIM) == 0
    nl = lb_logits.shape[0]

    def col(k):
        return pl.BlockSpec((1, T, width), lambda b, t: (b, t, k))

    return pl.pallas_call(
        functools.partial(_hgrn_kernel, layer=layer, heads=heads),
        grid=(bsz, seq // T),
        in_specs=[col(0), col(1), col(2), col(3),
                  pl.BlockSpec((nl, width), lambda b, t: (0, 0)),
                  pl.BlockSpec((1, width), lambda b, t: (0, 0))],
        out_specs=pl.BlockSpec((1, T, width), lambda b, t: (b, t, 0)),
        out_shape=jax.ShapeDtypeStruct((bsz, seq, width), BF16),
        scratch_shapes=[
            pltpu.VMEM((heads, HG_KDIM, HG_KDIM), F32),
            pltpu.VMEM((heads, T, HG_KDIM), F32),
            pltpu.VMEM((heads, T, HG_KDIM), F32),
            pltpu.VMEM((heads, T, HG_KDIM), F32),
            pltpu.VMEM((heads, T, HG_KDIM), F32),
            pltpu.VMEM((heads, T, HG_KDIM), F32),
            pltpu.VMEM((heads, T, HG_KDIM), F32),
            pltpu.VMEM((HG_PAR, heads // 2, HG_SUB * HG_SUB, 2 * HG_KDIM), BF16),
        ],
        compiler_params=_params("parallel", "arbitrary"),
        name="hgrn2",
    )(proj3, proj3, proj3, proj3, lb_logits, gnorm)


def _rglru_kernel(x_ref, gate_ref, cw_ref, cb_ref, wa_ref, ba_ref, wx_ref, bx_ref, lam_ref, y_ref,
                  tail_ref, h_ref, *, blocks):
    T = x_ref.shape[1]
    W = x_ref.shape[2]
    bw = W // blocks
    t_idx = pl.program_id(1)

    @pl.when(t_idx == 0)
    def _():
        tail_ref[...] = jnp.zeros_like(tail_ref)
        h_ref[...] = jnp.zeros_like(h_ref)

    x = x_ref[0]
    ext = jnp.concatenate([tail_ref[...], x], axis=0)
    xc = cb_ref[...] + x * cw_ref[0:1, :]
    for j in range(1, CONV_WIDTH):
        xc = xc + pltpu.roll(ext, j, 0)[SUBLANES:, :] * cw_ref[j:j + 1, :]
    tail_ref[...] = x[T - SUBLANES:, :]

    xcb = xc.astype(BF16)
    ra, rx = [], []
    for n in range(blocks):
        blk = xcb[:, n * bw:(n + 1) * bw]
        ra.append(_dot(blk, wa_ref[n]))
        rx.append(_dot(blk, wx_ref[n]))
    rg = _logistic(jnp.concatenate(ra, axis=-1) + ba_ref[...])
    ig = _logistic(jnp.concatenate(rx, axis=-1) + bx_ref[...])

    lam = lam_ref[...]
    log_sig = -(jnp.maximum(-lam, 0.0) + jnp.log1p(jnp.exp(-jnp.abs(lam))))
    log_a = LRU_C * rg * log_sig
    a = jnp.exp(log_a)
    y = -jnp.tanh(log_a) * (a * a + 1.0)
    mult = jnp.where(y > 0.0, y * lax.rsqrt(y), 0.0)
    rows = lax.broadcasted_iota(jnp.int32, (T, 1), 0)
    mult = jnp.where((rows == 0) & (t_idx == 0), 1.0, mult)
    u = xc * ig * mult

    groups = T // SUBLANES
    a3 = a.reshape(groups, SUBLANES, W)
    u3 = u.reshape(groups, SUBLANES, W)
    sub = lax.broadcasted_iota(jnp.int32, (1, SUBLANES, 1), 1)
    d = 1
    while d < SUBLANES:
        keep = sub >= d
        u3 = a3 * jnp.where(keep, pltpu.roll(u3, d, 1), 0.0) + u3
        a3 = a3 * jnp.where(keep, pltpu.roll(a3, d, 1), 1.0)
        d *= 2
    carry = h_ref[...]
    hs = []
    for g in range(groups):
        hg = a3[g] * carry + u3[g]
        hs.append(hg)
        carry = hg[SUBLANES - 1:SUBLANES, :]
    h = jnp.concatenate(hs, axis=0)
    h_ref[...] = carry

    y_ref[0] = (h * jax.nn.gelu(gate_ref[0])).astype(y_ref.dtype)


def _rglru(proj3, conv_w, conv_b, wa, ba, wx, bx, lam, *, width, col0, t_pref=256):
    bsz, seq, _ = proj3.shape
    blocks = wa.shape[0]
    T = _tile(seq, t_pref)
    row = pl.BlockSpec((1, width), lambda b, t: (0, 0))
    wspec = pl.BlockSpec(wa.shape, lambda b, t: (0, 0, 0))
    return pl.pallas_call(
        functools.partial(_rglru_kernel, blocks=blocks),
        grid=(bsz, seq // T),
        in_specs=[
            pl.BlockSpec((1, T, width), lambda b, t: (b, t, col0)),
            pl.BlockSpec((1, T, width), lambda b, t: (b, t, col0 + 1)),
            pl.BlockSpec((CONV_WIDTH, width), lambda b, t: (0, 0)),
            row, wspec, row, wspec, row, row,
        ],
        out_specs=pl.BlockSpec((1, T, width), lambda b, t: (b, t, 0)),
        out_shape=jax.ShapeDtypeStruct((bsz, seq, width), BF16),
        scratch_shapes=[pltpu.VMEM((SUBLANES, width), F32), pltpu.VMEM((1, width), F32)],
        compiler_params=_params("parallel", "arbitrary"),
        name="rglru",
    )(proj3, proj3, conv_w, conv_b, wa, ba, wx, bx, lam)


def _merge_kernel(x_ref, ya_ref, yb_ref, ga_ref, gb_ref, bg_ref, wa_ref, wb_ref, wo_ref, o_ref):
    ga = jax.nn.sigmoid(ga_ref[...] + bg_ref[0:1, :])
    gb = jax.nn.sigmoid(gb_ref[...] + bg_ref[1:2, :])
    merged = ga * _dot(ya_ref[...], wa_ref[...]) + gb * _dot(yb_ref[...], wb_ref[...])
    o_ref[...] = x_ref[...] + _dot(merged.astype(BF16), wo_ref[...])


def _merge(x2, ya2, yb2, proj2, b_gate, w_a, w_b, w_o, *, gate_col0, tm_pref=512):
    m, d = x2.shape
    wa_w, wb_w = ya2.shape[1], yb2.shape[1]
    tm = _tile(m, tm_pref)
    return pl.pallas_call(
        _merge_kernel,
        grid=(m // tm,),
        in_specs=[
            pl.BlockSpec((tm, d), lambda i: (i, 0)),
            pl.BlockSpec((tm, wa_w), lambda i: (i, 0)),
            pl.BlockSpec((tm, wb_w), lambda i: (i, 0)),
            pl.BlockSpec((tm, d), lambda i: (i, gate_col0)),
            pl.BlockSpec((tm, d), lambda i: (i, gate_col0 + 1)),
            _resident(b_gate.shape), _resident(w_a.shape), _resident(w_b.shape), _resident(w_o.shape),
        ],
        out_specs=pl.BlockSpec((tm, d), lambda i: (i, 0)),
        out_shape=jax.ShapeDtypeStruct((m, d), F32),
        compiler_params=_params("parallel"),
        name="merge",
    )(x2, ya2, yb2, proj2, proj2, b_gate, w_a, w_b, w_o)


def _xattn_kernel(x_ref, g_ref, wq_ref, k_ref, v_ref, wo_ref, o_ref):
    x = x_ref[0]
    d = x.shape[-1]
    hd = d // XA_HEADS
    q = _dot(_rms(x, g_ref[...]).astype(BF16), wq_ref[...])
    heads = [slice(h * hd, (h + 1) * hd) for h in range(XA_HEADS)]
    scores = [_dot_nt(q[:, ls].astype(BF16), k_ref[0, :, ls]) * (hd ** -0.5) for ls in heads]
    probs = []
    for s in scores:
        p = jnp.exp(s - jnp.max(s, axis=-1, keepdims=True))
        probs.append((p / jnp.sum(p, axis=-1, keepdims=True)).astype(BF16))
    outs = [_dot(p, v_ref[0, :, ls]) for p, ls in zip(probs, heads)]
    o = jnp.concatenate(outs, axis=-1).astype(BF16)
    o_ref[0] = x + _dot(o, wo_ref[...])


def _xattn(x3, gain, wq, kv3, wo, *, tm_pref=512):
    bsz, seq, d = x3.shape
    nm = kv3.shape[1]
    tm = _tile(seq, tm_pref)
    return pl.pallas_call(
        _xattn_kernel,
        grid=(bsz, seq // tm),
        in_specs=[
            pl.BlockSpec((1, tm, d), lambda b, t: (b, t, 0)),
            _resident(gain.shape), _resident(wq.shape),
            pl.BlockSpec((1, nm, d), lambda b, t: (b, 0, 0)),
            pl.BlockSpec((1, nm, d), lambda b, t: (b, 0, 1)),
            _resident(wo.shape),
        ],
        out_specs=pl.BlockSpec((1, tm, d), lambda b, t: (b, t, 0)),
        out_shape=jax.ShapeDtypeStruct((bsz, seq, d), F32),
        compiler_params=_params("parallel", "parallel"),
        name="xattn",
    )(x3, gain, wq, kv3, kv3, wo)


def kernel(x, mem, ffn1_norm, ffn1_w_up, ffn1_w_down, mix_norm, w_in, b_gate, hgrn_lb_logits, hgrn_norm, conv_w, conv_b, lru_wa, lru_ba, lru_wx, lru_bx, lru_lambda, w_branch_a, w_branch_b, w_out, xattn_norm, mem_norm, xattn_wq, xattn_wkv, xattn_wo, ffn2_norm, ffn2_w_up, ffn2_w_down, final_norm):
    bsz, seq, d = x.shape
    nm = mem.shape[1]
    depth = ffn1_norm.shape[0]
    hg_w = hgrn_norm.shape[1]
    lru_w = lru_lambda.shape[1]
    assert hg_w == lru_w and 4 * hg_w + 2 * lru_w == 3 * d
    bf = lambda w: w.astype(BF16)
    row = lambda v: v.reshape(1, -1)
    fgain = row(final_norm)

    x2 = x.reshape(bsz * seq, d)
    mem2 = mem.reshape(bsz * nm, d)
    for l in range(depth):
        last = l == depth - 1
        x2 = _ffn(x2, row(ffn1_norm[l]), bf(ffn1_w_up[l]), bf(ffn1_w_down[l]), fgain, final_norm=False)

        proj2 = _norm_proj(x2, row(mix_norm[l]), bf(w_in[l]), F32, name="in_proj", tn_pref=1280)
        proj3 = proj2.reshape(bsz, seq, -1)
        y_a = _hgrn(proj3, hgrn_lb_logits, row(hgrn_norm[l]), layer=l, width=hg_w)
        y_b = _rglru(proj3, conv_w[l], row(conv_b[l]), bf(lru_wa[l]), row(lru_ba[l]),
                     bf(lru_wx[l]), row(lru_bx[l]), row(lru_lambda[l]), width=lru_w, col0=4)
        x2 = _merge(x2, y_a.reshape(bsz * seq, hg_w), y_b.reshape(bsz * seq, lru_w), proj2,
                    b_gate[l], bf(w_branch_a[l]), bf(w_branch_b[l]), bf(w_out[l]), gate_col0=3)

        kv2 = _norm_proj(mem2, row(mem_norm[l]), bf(xattn_wkv[l]), BF16, name="kv_proj")
        x3 = _xattn(x2.reshape(bsz, seq, d), row(xattn_norm[l]), bf(xattn_wq[l]),
                    kv2.reshape(bsz, nm, 2 * d), bf(xattn_wo[l]))
        x2 = x3.reshape(bsz * seq, d)

        x2 = _ffn(x2, row(ffn2_norm[l]), bf(ffn2_w_up[l]), bf(ffn2_w_down[l]), fgain,
                  final_norm=last)
    return x2.reshape(bsz, seq, d)
```

```python
import functools

import jax
import jax.numpy as jnp
from jax import lax
from jax.experimental import pallas as pl
from jax.experimental.pallas import tpu as pltpu

EPS = 1e-6
HG_KDIM = 128
XA_HEADS = 4
LRU_C = 8.0
CONV_WIDTH = 4
LOG2E = 1.4426950408889634
LANES = 128
SUBLANES = 8
V7X_VMEM_BYTES = 64 * 1024 * 1024
VMEM_LIMIT = V7X_VMEM_BYTES - 4 * 1024 * 1024

HG_CHUNK = 64
HG_SUB = 16
HG_PAR = 4
FFN_SPLIT = 2

F32 = jnp.float32
BF16 = jnp.bfloat16


def _params(*semantics):
    return pltpu.CompilerParams(dimension_semantics=semantics, vmem_limit_bytes=VMEM_LIMIT)


def _tile(n, pref):
    t = min(n, pref)
    while n % t:
        t //= 2
    return t


def _rms(xf, g):
    ms = jnp.mean(xf * xf, axis=-1, keepdims=True)
    return xf * lax.rsqrt(ms + EPS) * g


def _dot(a, b):
    return jnp.dot(a, b, preferred_element_type=F32)


def _dot_nt(a, b):
    return lax.dot_general(a, b, (((1,), (1,)), ((), ())), preferred_element_type=F32)


def _dot_tn(a, b):
    return lax.dot_general(a, b, (((0,), (0,)), ((), ())), preferred_element_type=F32)


def _resident(shape):
    return pl.BlockSpec(shape, lambda *_: (0,) * len(shape), pipeline_mode=pl.Buffered(1))


def _ffn_kernel(x_ref, g_ref, wg_ref, wu_ref, wd_ref, fg_ref, o_ref, hn_ref, *, final_norm):
    j = pl.program_id(1)

    @pl.when(j == 0)
    def _():
        x = x_ref[...]
        hn_ref[...] = _rms(x, g_ref[...]).astype(BF16)
        o_ref[...] = x

    h = hn_ref[...]
    half = wg_ref.shape[1] // FFN_SPLIT
    acts = []
    for c in range(FFN_SPLIT):
        cs = slice(c * half, (c + 1) * half)
        gate = _dot(h, wg_ref[:, cs])
        up = _dot(h, wu_ref[:, cs])
        acts.append((0.5 * (gate * jax.nn.sigmoid(gate) * up)).astype(BF16))
    for c in range(FFN_SPLIT):
        o_ref[...] += _dot(acts[c], wd_ref[c * half:(c + 1) * half, :])

    if final_norm:
        @pl.when(j == pl.num_programs(1) - 1)
        def _():
            o_ref[...] = _rms(o_ref[...], fg_ref[...])


def _ffn(x2, gain, w_up, w_down, final_gain, *, final_norm, tm_pref=1024, tf_pref=512):
    m, d = x2.shape
    dff = w_down.shape[0]
    tm = _tile(m, tm_pref)
    tf = _tile(dff, tf_pref)
    nf = dff // tf
    return pl.pallas_call(
        functools.partial(_ffn_kernel, final_norm=final_norm),
        grid=(m // tm, nf),
        in_specs=[
            pl.BlockSpec((tm, d), lambda i, j: (i, 0)),
            pl.BlockSpec((1, d), lambda i, j: (0, 0)),
            pl.BlockSpec((d, tf), lambda i, j: (0, j)),
            pl.BlockSpec((d, tf), lambda i, j: (0, j + nf)),
            pl.BlockSpec((tf, d), lambda i, j: (j, 0)),
            pl.BlockSpec((1, d), lambda i, j: (0, 0)),
        ],
        out_specs=pl.BlockSpec((tm, d), lambda i, j: (i, 0)),
        out_shape=jax.ShapeDtypeStruct((m, d), F32),
        scratch_shapes=[pltpu.VMEM((tm, d), BF16)],
        compiler_params=_params("parallel", "arbitrary"),
        name="ffn_final" if final_norm else "ffn",
    )(x2, gain, w_up, w_up, w_down, final_gain)


def _norm_proj_kernel(x_ref, g_ref, w_ref, o_ref, hn_ref):
    @pl.when(pl.program_id(1) == 0)
    def _():
        hn_ref[...] = _rms(x_ref[...], g_ref[...]).astype(BF16)

    o_ref[...] = _dot(hn_ref[...], w_ref[...]).astype(o_ref.dtype)


def _norm_proj(x2, gain, w, out_dtype, *, name, tm_pref=1024, tn_pref=1024):
    m, d = x2.shape
    n = w.shape[1]
    tm = _tile(m, tm_pref)
    tn = _tile(n, tn_pref)
    return pl.pallas_call(
        _norm_proj_kernel,
        grid=(m // tm, n // tn),
        in_specs=[
            pl.BlockSpec((tm, d), lambda i, j: (i, 0)),
            pl.BlockSpec((1, d), lambda i, j: (0, 0)),
            pl.BlockSpec((d, tn), lambda i, j: (0, j)),
        ],
        out_specs=pl.BlockSpec((tm, tn), lambda i, j: (i, j)),
        out_shape=jax.ShapeDtypeStruct((m, n), out_dtype),
        scratch_shapes=[pltpu.VMEM((tm, d), BF16)],
        compiler_params=_params("parallel", "arbitrary"),
        name=name,
    )(x2, gain, w)


def _logistic(x):
    return 0.5 * jnp.tanh(0.5 * x) + 0.5


def _silu(x):
    h = 0.5 * x
    return h + h * jnp.tanh(h)


def _hgrn_kernel(q_ref, f_ref, v_ref, g_ref, lbl_ref, gn_ref, y_ref,
                 st_ref, qf_ref, kk_ref, vv_ref, b_ref, od_ref, oc_ref, es_ref, *, layer, heads):
    T = q_ref.shape[1]
    K = HG_KDIM
    H8 = SUBLANES

    @pl.when(pl.program_id(1) == 0)
    def _():
        st_ref[...] = jnp.zeros_like(st_ref)

    lg = lbl_ref[...]
    ex = jnp.exp(lg - jnp.max(lg, axis=0, keepdims=True))
    sm = ex / jnp.sum(ex, axis=0, keepdims=True)
    lb = jnp.sum(sm[:layer + 1], axis=0, keepdims=True)

    r = lax.broadcasted_iota(jnp.int32, (T, T), 0)
    c = lax.broadcasted_iota(jnp.int32, (T, T), 1)
    tri = jnp.where((c <= r) & (r // HG_CHUNK == c // HG_CHUNK), 1.0, 0.0).astype(BF16)
    f_mid = 0.5 * (1.0 + lb)
    f_half = 0.5 * (1.0 - lb)
    for h in range(heads):
        ls = slice(h * K, (h + 1) * K)
        qf_ref[h] = _silu(q_ref[0, :, ls])
        swing = f_half[:, ls] * jnp.tanh(0.5 * f_ref[0, :, ls])
        f = f_mid[:, ls] + swing
        kk_ref[h] = f_half[:, ls] - swing
        vv_ref[h] = v_ref[0, :, ls]
        logf = jnp.log(f) * LOG2E
        hi = logf.astype(BF16)
        lo = (logf - hi.astype(F32)).astype(BF16)
        b_ref[h] = _dot(tri, hi) + _dot(tri, lo)

    pr = lax.broadcasted_iota(jnp.int32, (2 * K, 2 * K), 0)
    pc = lax.broadcasted_iota(jnp.int32, (2 * K, 2 * K), 1)
    seg_ones = jnp.where(pr // K == pc // K, 1.0, 0.0).astype(BF16)
    trow = lax.broadcasted_iota(jnp.int32, (H8, 1), 0)
    zeros8 = jnp.zeros((H8, K), F32)

    def diag_block(r0, e_ref):
        for p in range(heads // 2):
            for hh in range(2):
                h = 2 * p + hh
                q_lo = qf_ref[h, pl.ds(r0, H8), :]
                q_hi = qf_ref[h, pl.ds(r0 + H8, H8), :]
                b_lo = b_ref[h, pl.ds(r0, H8), :]
                b_hi = b_ref[h, pl.ds(r0 + H8, H8), :]
                for s in range(HG_SUB):
                    ks = kk_ref[h, pl.ds(r0 + s, 1), :]
                    bs = b_ref[h, pl.ds(r0 + s, 1), :]
                    e_hi = jnp.exp2(b_hi - bs) * (q_hi * ks)
                    if s == 0:
                        e_lo = jnp.exp2(b_lo - bs) * (q_lo * ks)
                    elif s < H8:
                        e_lo = jnp.where(trow >= s, jnp.exp2(b_lo - bs) * (q_lo * ks), 0.0)
                    else:
                        e_lo = zeros8
                        if s > H8:
                            e_hi = jnp.where(trow >= s - H8, e_hi, 0.0)
                    e_ref[p, s * HG_SUB:(s + 1) * HG_SUB, hh * K:(hh + 1) * K] = (
                        jnp.concatenate([e_lo, e_hi], axis=0).astype(BF16))
            w = _dot(e_ref[p], seg_ones)
            for hh in range(2):
                h = 2 * p + hh
                o_lo = jnp.zeros((H8, K), F32)
                o_hi = jnp.zeros((H8, K), F32)
                for s in range(HG_SUB):
                    vs = vv_ref[h, pl.ds(r0 + s, 1), :]
                    if s < H8:
                        o_lo = o_lo + w[s * HG_SUB:s * HG_SUB + H8, hh * K:(hh + 1) * K] * vs
                    o_hi = o_hi + w[s * HG_SUB + H8:(s + 1) * HG_SUB, hh * K:(hh + 1) * K] * vs
                od_ref[h, pl.ds(r0, H8), :] = o_lo
                od_ref[h, pl.ds(r0 + H8, H8), :] = o_hi

    n_par = es_ref.shape[0]

    def diag_group(gi, carry):
        for u in range(n_par):
            diag_block(pl.multiple_of((gi * n_par + u) * HG_SUB, HG_SUB), es_ref.at[u])
        return carry

    lax.fori_loop(0, T // (HG_SUB * n_par), diag_group, 0)

    nsub = HG_CHUNK // HG_SUB

    states = [st_ref[h] for h in range(heads)]
    for ci in range(T // HG_CHUNK):
        r0 = ci * HG_CHUNK
        o_inter, scores, vbs = [], [], []
        for h in range(heads):
            bq = b_ref[h, pl.ds(r0, HG_CHUNK), :]
            qh = qf_ref[h, pl.ds(r0, HG_CHUNK), :]
            kh = kk_ref[h, pl.ds(r0, HG_CHUNK), :]
            vhb = vv_ref[h, pl.ds(r0, HG_CHUNK), :].astype(BF16)
            st = states[h]
            o_inter.append(_dot_nt((qh * jnp.exp2(bq)).astype(BF16), st.astype(BF16)))
            sc_h = []
            for blk in range(1, nsub):
                lo_r = blk * HG_SUB
                beta = bq[lo_r - 1:lo_r, :]
                qt = qh[lo_r:lo_r + HG_SUB] * jnp.exp2(bq[lo_r:lo_r + HG_SUB] - beta)
                kt = kh[:lo_r] * jnp.exp2(beta - bq[:lo_r])
                sc_h.append(_dot_nt(qt.astype(BF16), kt.astype(BF16)).astype(BF16))
            scores.append(sc_h)
            vbs.append(vhb)
            b_last = bq[HG_CHUNK - 1:HG_CHUNK, :]
            kd = (kh * jnp.exp2(b_last - bq)).astype(BF16)
            states[h] = st * jnp.exp2(b_last) + _dot_tn(vhb, kd)
        for h in range(heads):
            offs = [jnp.zeros((HG_SUB, K), F32)]
            for blk in range(1, nsub):
                offs.append(_dot(scores[h][blk - 1], vbs[h][:blk * HG_SUB]))
            oc_ref[h, pl.ds(r0, HG_CHUNK), :] = o_inter[h] + jnp.concatenate(offs, axis=0)
    for h in range(heads):
        st_ref[h] = states[h]

    for h in range(heads):
        ls = slice(h * K, (h + 1) * K)
        g = g_ref[0, :, ls]
        o = od_ref[h] + oc_ref[h]
        y_ref[0, :, ls] = (_rms(o, gn_ref[:, ls]) * _silu(g)).astype(y_ref.dtype)


def _hgrn(proj3, lb_logits, gnorm, *, layer, width, t_pref=256):
    bsz, seq, _ = proj3.shape
    heads = width // HG_KDIM
    T = _tile(seq, t_pref)
    assert T % HG_CHUNK == 0 and T % (HG_SUB * HG_PAR) == 0 and width % (2 * HG_KDIM) == 0
    nl = lb_logits.shape[0]

    def col(k):
        return pl.BlockSpec((1, T, width), lambda b, t: (b, t, k))

    return pl.pallas_call(
        functools.partial(_hgrn_kernel, layer=layer, heads=heads),
        grid=(bsz, seq // T),
        in_specs=[col(0), col(1), col(2), col(3),
                  pl.BlockSpec((nl, width), lambda b, t: (0, 0)),
                  pl.BlockSpec((1, width), lambda b, t: (0, 0))],
        out_specs=pl.BlockSpec((1, T, width), lambda b, t: (b, t, 0)),
        out_shape=jax.ShapeDtypeStruct((bsz, seq, width), BF16),
        scratch_shapes=[
            pltpu.VMEM((heads, HG_KDIM, HG_KDIM), F32),
            pltpu.VMEM((heads, T, HG_KDIM), F32),
            pltpu.VMEM((heads, T, HG_KDIM), F32),
            pltpu.VMEM((heads, T, HG_KDIM), F32),
            pltpu.VMEM((heads, T, HG_KDIM), F32),
            pltpu.VMEM((heads, T, HG_KDIM), F32),
            pltpu.VMEM((heads, T, HG_KDIM), F32),
            pltpu.VMEM((HG_PAR, heads // 2, HG_SUB * HG_SUB, 2 * HG_KDIM), BF16),
        ],
        compiler_params=_params("parallel", "arbitrary"),
        name="hgrn2",
    )(proj3, proj3, proj3, proj3, lb_logits, gnorm)


def _rglru_kernel(x_ref, gate_ref, cw_ref, cb_ref, wa_ref, ba_ref, wx_ref, bx_ref, lam_ref, y_ref,
                  tail_ref, h_ref, *, blocks):
    T = x_ref.shape[1]
    W = x_ref.shape[2]
    bw = W // blocks
    t_idx = pl.program_id(1)

    @pl.when(t_idx == 0)
    def _():
        tail_ref[...] = jnp.zeros_like(tail_ref)
        h_ref[...] = jnp.zeros_like(h_ref)

    x = x_ref[0]
    ext = jnp.concatenate([tail_ref[...], x], axis=0)
    xc = cb_ref[...] + x * cw_ref[0:1, :]
    for j in range(1, CONV_WIDTH):
        xc = xc + pltpu.roll(ext, j, 0)[SUBLANES:, :] * cw_ref[j:j + 1, :]
    tail_ref[...] = x[T - SUBLANES:, :]

    xcb = xc.astype(BF16)
    ra, rx = [], []
    for n in range(blocks):
        blk = xcb[:, n * bw:(n + 1) * bw]
        ra.append(_dot(blk, wa_ref[n]))
        rx.append(_dot(blk, wx_ref[n]))
    rg = _logistic(jnp.concatenate(ra, axis=-1) + ba_ref[...])
    ig = _logistic(jnp.concatenate(rx, axis=-1) + bx_ref[...])

    lam = lam_ref[...]
    log_sig = -(jnp.maximum(-lam, 0.0) + jnp.log1p(jnp.exp(-jnp.abs(lam))))
    log_a = LRU_C * rg * log_sig
    a = jnp.exp(log_a)
    y = -jnp.tanh(log_a) * (a * a + 1.0)
    mult = jnp.where(y > 0.0, y * lax.rsqrt(y), 0.0)
    rows = lax.broadcasted_iota(jnp.int32, (T, 1), 0)
    mult = jnp.where((rows == 0) & (t_idx == 0), 1.0, mult)
    u = xc * ig * mult

    groups = T // SUBLANES
    a3 = a.reshape(groups, SUBLANES, W)
    u3 = u.reshape(groups, SUBLANES, W)
    sub = lax.broadcasted_iota(jnp.int32, (1, SUBLANES, 1), 1)
    d = 1
    while d < SUBLANES:
        keep = sub >= d
        u3 = a3 * jnp.where(keep, pltpu.roll(u3, d, 1), 0.0) + u3
        a3 = a3 * jnp.where(keep, pltpu.roll(a3, d, 1), 1.0)
        d *= 2
    carry = h_ref[...]
    hs = []
    for g in range(groups):
        hg = a3[g] * carry + u3[g]
        hs.append(hg)
        carry = hg[SUBLANES - 1:SUBLANES, :]
    h = jnp.concatenate(hs, axis=0)
    h_ref[...] = carry

    y_ref[0] = (h * jax.nn.gelu(gate_ref[0])).astype(y_ref.dtype)


def _rglru(proj3, conv_w, conv_b, wa, ba, wx, bx, lam, *, width, col0, t_pref=256):
    bsz, seq, _ = proj3.shape
    blocks = wa.shape[0]
    T = _tile(seq, t_pref)
    row = pl.BlockSpec((1, width), lambda b, t: (0, 0))
    wspec = pl.BlockSpec(wa.shape, lambda b, t: (0, 0, 0))
    return pl.pallas_call(
        functools.partial(_rglru_kernel, blocks=blocks),
        grid=(bsz, seq // T),
        in_specs=[
            pl.BlockSpec((1, T, width), lambda b, t: (b, t, col0)),
            pl.BlockSpec((1, T, width), lambda b, t: (b, t, col0 + 1)),
            pl.BlockSpec((CONV_WIDTH, width), lambda b, t: (0, 0)),
            row, wspec, row, wspec, row, row,
        ],
        out_specs=pl.BlockSpec((1, T, width), lambda b, t: (b, t, 0)),
        out_shape=jax.ShapeDtypeStruct((bsz, seq, width), BF16),
        scratch_shapes=[pltpu.VMEM((SUBLANES, width), F32), pltpu.VMEM((1, width), F32)],
        compiler_params=_params("parallel", "arbitrary"),
        name="rglru",
    )(proj3, proj3, conv_w, conv_b, wa, ba, wx, bx, lam)


def _merge_kernel(x_ref, ya_ref, yb_ref, ga_ref, gb_ref, bg_ref, wa_ref, wb_ref, wo_ref, o_ref):
    ga = jax.nn.sigmoid(ga_ref[...] + bg_ref[0:1, :])
    gb = jax.nn.sigmoid(gb_ref[...] + bg_ref[1:2, :])
    merged = ga * _dot(ya_ref[...], wa_ref[...]) + gb * _dot(yb_ref[...], wb_ref[...])
    o_ref[...] = x_ref[...] + _dot(merged.astype(BF16), wo_ref[...])


def _merge(x2, ya2, yb2, proj2, b_gate, w_a, w_b, w_o, *, gate_col0, tm_pref=512):
    m, d = x2.shape
    wa_w, wb_w = ya2.shape[1], yb2.shape[1]
    tm = _tile(m, tm_pref)
    return pl.pallas_call(
        _merge_kernel,
        grid=(m // tm,),
        in_specs=[
            pl.BlockSpec((tm, d), lambda i: (i, 0)),
            pl.BlockSpec((tm, wa_w), lambda i: (i, 0)),
            pl.BlockSpec((tm, wb_w), lambda i: (i, 0)),
            pl.BlockSpec((tm, d), lambda i: (i, gate_col0)),
            pl.BlockSpec((tm, d), lambda i: (i, gate_col0 + 1)),
            _resident(b_gate.shape), _resident(w_a.shape), _resident(w_b.shape), _resident(w_o.shape),
        ],
        out_specs=pl.BlockSpec((tm, d), lambda i: (i, 0)),
        out_shape=jax.ShapeDtypeStruct((m, d), F32),
        compiler_params=_params("parallel"),
        name="merge",
    )(x2, ya2, yb2, proj2, proj2, b_gate, w_a, w_b, w_o)


def _xattn_kernel(x_ref, g_ref, wq_ref, k_ref, v_ref, wo_ref, o_ref):
    x = x_ref[0]
    d = x.shape[-1]
    hd = d // XA_HEADS
    q = _dot(_rms(x, g_ref[...]).astype(BF16), wq_ref[...])
    heads = [slice(h * hd, (h + 1) * hd) for h in range(XA_HEADS)]
    scores = [_dot_nt(q[:, ls].astype(BF16), k_ref[0, :, ls]) * (hd ** -0.5) for ls in heads]
    probs = []
    for s in scores:
        p = jnp.exp(s - jnp.max(s, axis=-1, keepdims=True))
        probs.append((p / jnp.sum(p, axis=-1, keepdims=True)).astype(BF16))
    outs = [_dot(p, v_ref[0, :, ls]) for p, ls in zip(probs, heads)]
    o = jnp.concatenate(outs, axis=-1).astype(BF16)
    o_ref[0] = x + _dot(o, wo_ref[...])


def _xattn(x3, gain, wq, kv3, wo, *, tm_pref=512):
    bsz, seq, d = x3.shape
    nm = kv3.shape[1]
    tm = _tile(seq, tm_pref)
    return pl.pallas_call(
        _xattn_kernel,
        grid=(bsz, seq // tm),
        in_specs=[
            pl.BlockSpec((1, tm, d), lambda b, t: (b, t, 0)),
            _resident(gain.shape), _resident(wq.shape),
            pl.BlockSpec((1, nm, d), lambda b, t: (b, 0, 0)),
            pl.BlockSpec((1, nm, d), lambda b, t: (b, 0, 1)),
            _resident(wo.shape),
        ],
        out_specs=pl.BlockSpec((1, tm, d), lambda b, t: (b, t, 0)),
        out_shape=jax.ShapeDtypeStruct((bsz, seq, d), F32),
        compiler_params=_params("parallel", "parallel"),
        name="xattn",
    )(x3, gain, wq, kv3, kv3, wo)


def kernel(x, mem, ffn1_norm, ffn1_w_up, ffn1_w_down, mix_norm, w_in, b_gate, hgrn_lb_logits, hgrn_norm, conv_w, conv_b, lru_wa, lru_ba, lru_wx, lru_bx, lru_lambda, w_branch_a, w_branch_b, w_out, xattn_norm, mem_norm, xattn_wq, xattn_wkv, xattn_wo, ffn2_norm, ffn2_w_up, ffn2_w_down, final_norm):
    bsz, seq, d = x.shape
    nm = mem.shape[1]
    depth = ffn1_norm.shape[0]
    hg_w = hgrn_norm.shape[1]
    lru_w = lru_lambda.shape[1]
    assert hg_w == lru_w and 4 * hg_w + 2 * lru_w == 3 * d
    bf = lambda w: w.astype(BF16)
    row = lambda v: v.reshape(1, -1)
    fgain = row(final_norm)

    x2 = x.reshape(bsz * seq, d)
    mem2 = mem.reshape(bsz * nm, d)
    for l in range(depth):
        last = l == depth - 1
        x2 = _ffn(x2, row(ffn1_norm[l]), bf(ffn1_w_up[l]), bf(ffn1_w_down[l]), fgain, final_norm=False)

        proj2 = _norm_proj(x2, row(mix_norm[l]), bf(w_in[l]), F32, name="in_proj", tn_pref=2048)
        proj3 = proj2.reshape(bsz, seq, -1)
        y_a = _hgrn(proj3, hgrn_lb_logits, row(hgrn_norm[l]), layer=l, width=hg_w)
        y_b = _rglru(proj3, conv_w[l], row(conv_b[l]), bf(lru_wa[l]), row(lru_ba[l]),
                     bf(lru_wx[l]), row(lru_bx[l]), row(lru_lambda[l]), width=lru_w, col0=4)
        x2 = _merge(x2, y_a.reshape(bsz * seq, hg_w), y_b.reshape(bsz * seq, lru_w), proj2,
                    b_gate[l], bf(w_branch_a[l]), bf(w_branch_b[l]), bf(w_out[l]), gate_col0=3)

        kv2 = _norm_proj(mem2, row(mem_norm[l]), bf(xattn_wkv[l]), BF16, name="kv_proj")
        x3 = _xattn(x2.reshape(bsz, seq, d), row(xattn_norm[l]), bf(xattn_wq[l]),
                    kv2.reshape(bsz, nm, 2 * d), bf(xattn_wo[l]))
        x2 = x3.reshape(bsz * seq, d)

        x2 = _ffn(x2, row(ffn2_norm[l]), bf(ffn2_w_up[l]), bf(ffn2_w_down[l]), fgain,
                  final_norm=last)
    return x2.reshape(bsz, seq, d)
```

```python
import functools

import jax
import jax.numpy as jnp
from jax import lax
from jax.experimental import pallas as pl
from jax.experimental.pallas import tpu as pltpu

EPS = 1e-6
HG_KDIM = 128
XA_HEADS = 4
LRU_C = 8.0
CONV_WIDTH = 4
LOG2E = 1.4426950408889634
LANES = 128
SUBLANES = 8
V7X_VMEM_BYTES = 64 * 1024 * 1024
VMEM_LIMIT = V7X_VMEM_BYTES - 4 * 1024 * 1024

HG_CHUNK = 64
HG_SUB = 16
HG_PAR = 4
FFN_SPLIT = 2

F32 = jnp.float32
BF16 = jnp.bfloat16


def _params(*semantics):
    return pltpu.CompilerParams(dimension_semantics=semantics, vmem_limit_bytes=VMEM_LIMIT)


def _tile(n, pref):
    t = min(n, pref)
    while n % t:
        t //= 2
    return t


def _rms(xf, g):
    ms = jnp.mean(xf * xf, axis=-1, keepdims=True)
    return xf * lax.rsqrt(ms + EPS) * g


def _dot(a, b):
    return jnp.dot(a, b, preferred_element_type=F32)


def _dot_nt(a, b):
    return lax.dot_general(a, b, (((1,), (1,)), ((), ())), preferred_element_type=F32)


def _dot_tn(a, b):
    return lax.dot_general(a, b, (((0,), (0,)), ((), ())), preferred_element_type=F32)


def _resident(shape):
    return pl.BlockSpec(shape, lambda *_: (0,) * len(shape), pipeline_mode=pl.Buffered(1))


def _ffn_kernel(x_ref, g_ref, wg_ref, wu_ref, wd_ref, fg_ref, o_ref, hn_ref, *, final_norm):
    j = pl.program_id(1)

    @pl.when(j == 0)
    def _():
        x = x_ref[...]
        hn_ref[...] = _rms(x, g_ref[...]).astype(BF16)
        o_ref[...] = x

    h = hn_ref[...]
    half = wg_ref.shape[1] // FFN_SPLIT
    acts = []
    for c in range(FFN_SPLIT):
        cs = slice(c * half, (c + 1) * half)
        gate = _dot(h, wg_ref[:, cs])
        up = _dot(h, wu_ref[:, cs])
        acts.append((0.5 * (gate * jax.nn.sigmoid(gate) * up)).astype(BF16))
    for c in range(FFN_SPLIT):
        o_ref[...] += _dot(acts[c], wd_ref[c * half:(c + 1) * half, :])

    if final_norm:
        @pl.when(j == pl.num_programs(1) - 1)
        def _():
            o_ref[...] = _rms(o_ref[...], fg_ref[...])


def _ffn(x2, gain, w_up, w_down, final_gain, *, final_norm, tm_pref=1024, tf_pref=512):
    m, d = x2.shape
    dff = w_down.shape[0]
    tm = _tile(m, tm_pref)
    tf = _tile(dff, tf_pref)
    nf = dff // tf
    return pl.pallas_call(
        functools.partial(_ffn_kernel, final_norm=final_norm),
        grid=(m // tm, nf),
        in_specs=[
            pl.BlockSpec((tm, d), lambda i, j: (i, 0)),
            pl.BlockSpec((1, d), lambda i, j: (0, 0)),
            pl.BlockSpec((d, tf), lambda i, j: (0, j)),
            pl.BlockSpec((d, tf), lambda i, j: (0, j + nf)),
            pl.BlockSpec((tf, d), lambda i, j: (j, 0)),
            pl.BlockSpec((1, d), lambda i, j: (0, 0)),
        ],
        out_specs=pl.BlockSpec((tm, d), lambda i, j: (i, 0)),
        out_shape=jax.ShapeDtypeStruct((m, d), F32),
        scratch_shapes=[pltpu.VMEM((tm, d), BF16)],
        compiler_params=_params("parallel", "arbitrary"),
        name="ffn_final" if final_norm else "ffn",
    )(x2, gain, w_up, w_up, w_down, final_gain)


def _norm_proj_kernel(x_ref, g_ref, w_ref, o_ref, hn_ref):
    @pl.when(pl.program_id(1) == 0)
    def _():
        hn_ref[...] = _rms(x_ref[...], g_ref[...]).astype(BF16)

    o_ref[...] = _dot(hn_ref[...], w_ref[...]).astype(o_ref.dtype)


def _norm_proj(x2, gain, w, out_dtype, *, name, tm_pref=1024, tn_pref=1024):
    m, d = x2.shape
    n = w.shape[1]
    tm = _tile(m, tm_pref)
    tn = _tile(n, tn_pref)
    return pl.pallas_call(
        _norm_proj_kernel,
        grid=(m // tm, n // tn),
        in_specs=[
            pl.BlockSpec((tm, d), lambda i, j: (i, 0)),
            pl.BlockSpec((1, d), lambda i, j: (0, 0)),
            pl.BlockSpec((d, tn), lambda i, j: (0, j)),
        ],
        out_specs=pl.BlockSpec((tm, tn), lambda i, j: (i, j)),
        out_shape=jax.ShapeDtypeStruct((m, n), out_dtype),
        scratch_shapes=[pltpu.VMEM((tm, d), BF16)],
        compiler_params=_params("parallel", "arbitrary"),
        name=name,
    )(x2, gain, w)


def _logistic(x):
    return 0.5 * jnp.tanh(0.5 * x) + 0.5


def _silu(x):
    h = 0.5 * x
    return h + h * jnp.tanh(h)


def _hgrn_kernel(q_ref, f_ref, v_ref, g_ref, lbl_ref, gn_ref, y_ref,
                 st_ref, qf_ref, kk_ref, vv_ref, b_ref, od_ref, oc_ref, *, layer, heads):
    T = q_ref.shape[1]
    K = HG_KDIM
    H8 = SUBLANES

    @pl.when(pl.program_id(1) == 0)
    def _():
        st_ref[...] = jnp.zeros_like(st_ref)

    lg = lbl_ref[...]
    ex = jnp.exp(lg - jnp.max(lg, axis=0, keepdims=True))
    sm = ex / jnp.sum(ex, axis=0, keepdims=True)
    lb = jnp.sum(sm[:layer + 1], axis=0, keepdims=True)

    r = lax.broadcasted_iota(jnp.int32, (T, T), 0)
    c = lax.broadcasted_iota(jnp.int32, (T, T), 1)
    tri = jnp.where((c <= r) & (r // HG_CHUNK == c // HG_CHUNK), 1.0, 0.0).astype(BF16)
    f_mid = 0.5 * (1.0 + lb)
    f_half = 0.5 * (1.0 - lb)
    for h in range(heads):
        ls = slice(h * K, (h + 1) * K)
        qf_ref[h] = _silu(q_ref[0, :, ls])
        swing = f_half[:, ls] * jnp.tanh(0.5 * f_ref[0, :, ls])
        f = f_mid[:, ls] + swing
        kk_ref[h] = f_half[:, ls] - swing
        vv_ref[h] = v_ref[0, :, ls]
        logf = jnp.log(f) * LOG2E
        hi = logf.astype(BF16)
        lo = (logf - hi.astype(F32)).astype(BF16)
        b_ref[h] = _dot(tri, hi) + _dot(tri, lo)

    pr = lax.broadcasted_iota(jnp.int32, (2 * K, 2 * K), 0)
    pc = lax.broadcasted_iota(jnp.int32, (2 * K, 2 * K), 1)
    seg_ones = jnp.where(pr // K == pc // K, 1.0, 0.0).astype(BF16)
    trow = lax.broadcasted_iota(jnp.int32, (H8, 1), 0)
    zeros8 = jnp.zeros((H8, K), F32)

    def diag_block(r0):
        for p in range(heads // 2):
            slabs = [[None, None] for _ in range(HG_SUB)]
            for hh in range(2):
                h = 2 * p + hh
                q_lo = qf_ref[h, pl.ds(r0, H8), :]
                q_hi = qf_ref[h, pl.ds(r0 + H8, H8), :]
                b_lo = b_ref[h, pl.ds(r0, H8), :]
                b_hi = b_ref[h, pl.ds(r0 + H8, H8), :]
                for s in range(HG_SUB):
                    ks = kk_ref[h, pl.ds(r0 + s, 1), :]
                    bs = b_ref[h, pl.ds(r0 + s, 1), :]
                    e_hi = jnp.exp2(b_hi - bs) * (q_hi * ks)
                    if s == 0:
                        e_lo = jnp.exp2(b_lo - bs) * (q_lo * ks)
                    elif s < H8:
                        e_lo = jnp.where(trow >= s, jnp.exp2(b_lo - bs) * (q_lo * ks), 0.0)
                    else:
                        e_lo = zeros8
                        if s > H8:
                            e_hi = jnp.where(trow >= s - H8, e_hi, 0.0)
                    slabs[s][hh] = jnp.concatenate([e_lo, e_hi], axis=0).astype(BF16)
            pairs = jnp.concatenate([jnp.concatenate(sl, axis=1) for sl in slabs], axis=0)
            w = _dot(pairs, seg_ones)
            for hh in range(2):
                h = 2 * p + hh
                o_lo = jnp.zeros((H8, K), F32)
                o_hi = jnp.zeros((H8, K), F32)
                for s in range(HG_SUB):
                    vs = vv_ref[h, pl.ds(r0 + s, 1), :]
                    if s < H8:
                        o_lo = o_lo + w[s * HG_SUB:s * HG_SUB + H8, hh * K:(hh + 1) * K] * vs
                    o_hi = o_hi + w[s * HG_SUB + H8:(s + 1) * HG_SUB, hh * K:(hh + 1) * K] * vs
                od_ref[h, pl.ds(r0, H8), :] = o_lo
                od_ref[h, pl.ds(r0 + H8, H8), :] = o_hi

    def diag_group(gi, carry):
        for u in range(HG_PAR):
            diag_block(pl.multiple_of((gi * HG_PAR + u) * HG_SUB, HG_SUB))
        return carry

    lax.fori_loop(0, T // (HG_SUB * HG_PAR), diag_group, 0)

    nsub = HG_CHUNK // HG_SUB

    states = [st_ref[h] for h in range(heads)]
    for ci in range(T // HG_CHUNK):
        r0 = ci * HG_CHUNK
        o_inter, scores, vbs = [], [], []
        for h in range(heads):
            bq = b_ref[h, pl.ds(r0, HG_CHUNK), :]
            qh = qf_ref[h, pl.ds(r0, HG_CHUNK), :]
            kh = kk_ref[h, pl.ds(r0, HG_CHUNK), :]
            vhb = vv_ref[h, pl.ds(r0, HG_CHUNK), :].astype(BF16)
            st = states[h]
            o_inter.append(_dot_nt((qh * jnp.exp2(bq)).astype(BF16), st.astype(BF16)))
            sc_h = []
            for blk in range(1, nsub):
                lo_r = blk * HG_SUB
                beta = bq[lo_r - 1:lo_r, :]
                qt = qh[lo_r:lo_r + HG_SUB] * jnp.exp2(bq[lo_r:lo_r + HG_SUB] - beta)
                kt = kh[:lo_r] * jnp.exp2(beta - bq[:lo_r])
                sc_h.append(_dot_nt(qt.astype(BF16), kt.astype(BF16)).astype(BF16))
            scores.append(sc_h)
            vbs.append(vhb)
            b_last = bq[HG_CHUNK - 1:HG_CHUNK, :]
            kd = (kh * jnp.exp2(b_last - bq)).astype(BF16)
            states[h] = st * jnp.exp2(b_last) + _dot_tn(vhb, kd)
        for h in range(heads):
            offs = [jnp.zeros((HG_SUB, K), F32)]
            for blk in range(1, nsub):
                offs.append(_dot(scores[h][blk - 1], vbs[h][:blk * HG_SUB]))
            oc_ref[h, pl.ds(r0, HG_CHUNK), :] = o_inter[h] + jnp.concatenate(offs, axis=0)
    for h in range(heads):
        st_ref[h] = states[h]

    for h in range(heads):
        ls = slice(h * K, (h + 1) * K)
        g = g_ref[0, :, ls]
        o = od_ref[h] + oc_ref[h]
        y_ref[0, :, ls] = (_rms(o, gn_ref[:, ls]) * _silu(g)).astype(y_ref.dtype)


def _hgrn(proj3, lb_logits, gnorm, *, layer, width, t_pref=256):
    bsz, seq, _ = proj3.shape
    heads = width // HG_KDIM
    T = _tile(seq, t_pref)
    assert T % HG_CHUNK == 0 and T % (HG_SUB * HG_PAR) == 0 and width % (2 * HG_KDIM) == 0
    nl = lb_logits.shape[0]

    def col(k):
        return pl.BlockSpec((1, T, width), lambda b, t: (b, t, k))

    return pl.pallas_call(
        functools.partial(_hgrn_kernel, layer=layer, heads=heads),
        grid=(bsz, seq // T),
        in_specs=[col(0), col(1), col(2), col(3),
                  pl.BlockSpec((nl, width), lambda b, t: (0, 0)),
                  pl.BlockSpec((1, width), lambda b, t: (0, 0))],
        out_specs=pl.BlockSpec((1, T, width), lambda b, t: (b, t, 0)),
        out_shape=jax.ShapeDtypeStruct((bsz, seq, width), BF16),
        scratch_shapes=[
            pltpu.VMEM((heads, HG_KDIM, HG_KDIM), F32),
            pltpu.VMEM((heads, T, HG_KDIM), F32),
            pltpu.VMEM((heads, T, HG_KDIM), F32),
            pltpu.VMEM((heads, T, HG_KDIM), F32),
            pltpu.VMEM((heads, T, HG_KDIM), F32),
            pltpu.VMEM((heads, T, HG_KDIM), F32),
            pltpu.VMEM((heads, T, HG_KDIM), F32),
        ],
        compiler_params=_params("parallel", "arbitrary"),
        name="hgrn2",
    )(proj3, proj3, proj3, proj3, lb_logits, gnorm)


def _rglru_kernel(x_ref, gate_ref, cw_ref, cb_ref, wa_ref, ba_ref, wx_ref, bx_ref, lam_ref, y_ref,
                  tail_ref, h_ref, *, blocks):
    T = x_ref.shape[1]
    W = x_ref.shape[2]
    bw = W // blocks
    t_idx = pl.program_id(1)

    @pl.when(t_idx == 0)
    def _():
        tail_ref[...] = jnp.zeros_like(tail_ref)
        h_ref[...] = jnp.zeros_like(h_ref)

    x = x_ref[0]
    ext = jnp.concatenate([tail_ref[...], x], axis=0)
    xc = cb_ref[...] + x * cw_ref[0:1, :]
    for j in range(1, CONV_WIDTH):
        xc = xc + pltpu.roll(ext, j, 0)[SUBLANES:, :] * cw_ref[j:j + 1, :]
    tail_ref[...] = x[T - SUBLANES:, :]

    xcb = xc.astype(BF16)
    ra, rx = [], []
    for n in range(blocks):
        blk = xcb[:, n * bw:(n + 1) * bw]
        ra.append(_dot(blk, wa_ref[n]))
        rx.append(_dot(blk, wx_ref[n]))
    rg = _logistic(jnp.concatenate(ra, axis=-1) + ba_ref[...])
    ig = _logistic(jnp.concatenate(rx, axis=-1) + bx_ref[...])

    lam = lam_ref[...]
    log_sig = -(jnp.maximum(-lam, 0.0) + jnp.log1p(jnp.exp(-jnp.abs(lam))))
    log_a = LRU_C * rg * log_sig
    a = jnp.exp(log_a)
    y = -jnp.tanh(log_a) * (a * a + 1.0)
    mult = jnp.where(y > 0.0, y * lax.rsqrt(y), 0.0)
    rows = lax.broadcasted_iota(jnp.int32, (T, 1), 0)
    mult = jnp.where((rows == 0) & (t_idx == 0), 1.0, mult)
    u = xc * ig * mult

    groups = T // SUBLANES
    a3 = a.reshape(groups, SUBLANES, W)
    u3 = u.reshape(groups, SUBLANES, W)
    sub = lax.broadcasted_iota(jnp.int32, (1, SUBLANES, 1), 1)
    d = 1
    while d < SUBLANES:
        keep = sub >= d
        u3 = a3 * jnp.where(keep, pltpu.roll(u3, d, 1), 0.0) + u3
        a3 = a3 * jnp.where(keep, pltpu.roll(a3, d, 1), 1.0)
        d *= 2
    carry = h_ref[...]
    hs = []
    for g in range(groups):
        hg = a3[g] * carry + u3[g]
        hs.append(hg)
        carry = hg[SUBLANES - 1:SUBLANES, :]
    h = jnp.concatenate(hs, axis=0)
    h_ref[...] = carry

    y_ref[0] = (h * jax.nn.gelu(gate_ref[0])).astype(y_ref.dtype)


def _rglru(proj3, conv_w, conv_b, wa, ba, wx, bx, lam, *, width, col0, t_pref=256):
    bsz, seq, _ = proj3.shape
    blocks = wa.shape[0]
    T = _tile(seq, t_pref)
    row = pl.BlockSpec((1, width), lambda b, t: (0, 0))
    wspec = pl.BlockSpec(wa.shape, lambda b, t: (0, 0, 0))
    return pl.pallas_call(
        functools.partial(_rglru_kernel, blocks=blocks),
        grid=(bsz, seq // T),
        in_specs=[
            pl.BlockSpec((1, T, width), lambda b, t: (b, t, col0)),
            pl.BlockSpec((1, T, width), lambda b, t: (b, t, col0 + 1)),
            pl.BlockSpec((CONV_WIDTH, width), lambda b, t: (0, 0)),
            row, wspec, row, wspec, row, row,
        ],
        out_specs=pl.BlockSpec((1, T, width), lambda b, t: (b, t, 0)),
        out_shape=jax.ShapeDtypeStruct((bsz, seq, width), BF16),
        scratch_shapes=[pltpu.VMEM((SUBLANES, width), F32), pltpu.VMEM((1, width), F32)],
        compiler_params=_params("parallel", "arbitrary"),
        name="rglru",
    )(proj3, proj3, conv_w, conv_b, wa, ba, wx, bx, lam)


def _merge_kernel(x_ref, ya_ref, yb_ref, ga_ref, gb_ref, bg_ref, wa_ref, wb_ref, wo_ref, o_ref):
    ga = jax.nn.sigmoid(ga_ref[...] + bg_ref[0:1, :])
    gb = jax.nn.sigmoid(gb_ref[...] + bg_ref[1:2, :])
    merged = ga * _dot(ya_ref[...], wa_ref[...]) + gb * _dot(yb_ref[...], wb_ref[...])
    o_ref[...] = x_ref[...] + _dot(merged.astype(BF16), wo_ref[...])


def _merge(x2, ya2, yb2, proj2, b_gate, w_a, w_b, w_o, *, gate_col0, tm_pref=512):
    m, d = x2.shape
    wa_w, wb_w = ya2.shape[1], yb2.shape[1]
    tm = _tile(m, tm_pref)
    return pl.pallas_call(
        _merge_kernel,
        grid=(m // tm,),
        in_specs=[
            pl.BlockSpec((tm, d), lambda i: (i, 0)),
            pl.BlockSpec((tm, wa_w), lambda i: (i, 0)),
            pl.BlockSpec((tm, wb_w), lambda i: (i, 0)),
            pl.BlockSpec((tm, d), lambda i: (i, gate_col0)),
            pl.BlockSpec((tm, d), lambda i: (i, gate_col0 + 1)),
            _resident(b_gate.shape), _resident(w_a.shape), _resident(w_b.shape), _resident(w_o.shape),
        ],
        out_specs=pl.BlockSpec((tm, d), lambda i: (i, 0)),
        out_shape=jax.ShapeDtypeStruct((m, d), F32),
        compiler_params=_params("parallel"),
        name="merge",
    )(x2, ya2, yb2, proj2, proj2, b_gate, w_a, w_b, w_o)


def _xattn_kernel(x_ref, g_ref, wq_ref, k_ref, v_ref, wo_ref, o_ref):
    x = x_ref[0]
    d = x.shape[-1]
    hd = d // XA_HEADS
    q = _dot(_rms(x, g_ref[...]).astype(BF16), wq_ref[...])
    heads = [slice(h * hd, (h + 1) * hd) for h in range(XA_HEADS)]
    scores = [_dot_nt(q[:, ls].astype(BF16), k_ref[0, :, ls]) * (hd ** -0.5) for ls in heads]
    probs = []
    for s in scores:
        p = jnp.exp(s - jnp.max(s, axis=-1, keepdims=True))
        probs.append((p / jnp.sum(p, axis=-1, keepdims=True)).astype(BF16))
    outs = [_dot(p, v_ref[0, :, ls]) for p, ls in zip(probs, heads)]
    o = jnp.concatenate(outs, axis=-1).astype(BF16)
    o_ref[0] = x + _dot(o, wo_ref[...])


def _xattn(x3, gain, wq, kv3, wo, *, tm_pref=512):
    bsz, seq, d = x3.shape
    nm = kv3.shape[1]
    tm = _tile(seq, tm_pref)
    return pl.pallas_call(
        _xattn_kernel,
        grid=(bsz, seq // tm),
        in_specs=[
            pl.BlockSpec((1, tm, d), lambda b, t: (b, t, 0)),
            _resident(gain.shape), _resident(wq.shape),
            pl.BlockSpec((1, nm, d), lambda b, t: (b, 0, 0)),
            pl.BlockSpec((1, nm, d), lambda b, t: (b, 0, 1)),
            _resident(wo.shape),
        ],
        out_specs=pl.BlockSpec((1, tm, d), lambda b, t: (b, t, 0)),
        out_shape=jax.ShapeDtypeStruct((bsz, seq, d), F32),
        compiler_params=_params("parallel", "parallel"),
        name="xattn",
    )(x3, gain, wq, kv3, kv3, wo)


def kernel(x, mem, ffn1_norm, ffn1_w_up, ffn1_w_down, mix_norm, w_in, b_gate, hgrn_lb_logits, hgrn_norm, conv_w, conv_b, lru_wa, lru_ba, lru_wx, lru_bx, lru_lambda, w_branch_a, w_branch_b, w_out, xattn_norm, mem_norm, xattn_wq, xattn_wkv, xattn_wo, ffn2_norm, ffn2_w_up, ffn2_w_down, final_norm):
    bsz, seq, d = x.shape
    nm = mem.shape[1]
    depth = ffn1_norm.shape[0]
    hg_w = hgrn_norm.shape[1]
    lru_w = lru_lambda.shape[1]
    assert hg_w == lru_w and 4 * hg_w + 2 * lru_w == 3 * d
    bf = lambda w: w.astype(BF16)
    row = lambda v: v.reshape(1, -1)
    fgain = row(final_norm)

    x2 = x.reshape(bsz * seq, d)
    mem2 = mem.reshape(bsz * nm, d)
    for l in range(depth):
        last = l == depth - 1
        x2 = _ffn(x2, row(ffn1_norm[l]), bf(ffn1_w_up[l]), bf(ffn1_w_down[l]), fgain, final_norm=False)

        proj2 = _norm_proj(x2, row(mix_norm[l]), bf(w_in[l]), F32, name="in_proj", tn_pref=2048)
        proj3 = proj2.reshape(bsz, seq, -1)
        y_a = _hgrn(proj3, hgrn_lb_logits, row(hgrn_norm[l]), layer=l, width=hg_w)
        y_b = _rglru(proj3, conv_w[l], row(conv_b[l]), bf(lru_wa[l]), row(lru_ba[l]),
                     bf(lru_wx[l]), row(lru_bx[l]), row(lru_lambda[l]), width=lru_w, col0=4)
        x2 = _merge(x2, y_a.reshape(bsz * seq, hg_w), y_b.reshape(bsz * seq, lru_w), proj2,
                    b_gate[l], bf(w_branch_a[l]), bf(w_branch_b[l]), bf(w_out[l]), gate_col0=3)

        kv2 = _norm_proj(mem2, row(mem_norm[l]), bf(xattn_wkv[l]), BF16, name="kv_proj")
        x3 = _xattn(x2.reshape(bsz, seq, d), row(xattn_norm[l]), bf(xattn_wq[l]),
                    kv2.reshape(bsz, nm, 2 * d), bf(xattn_wo[l]))
        x2 = x3.reshape(bsz * seq, d)

        x2 = _ffn(x2, row(ffn2_norm[l]), bf(ffn2_w_up[l]), bf(ffn2_w_down[l]), fgain,
                  final_norm=last)
    return x2.reshape(bsz, seq, d)
```
